```python
import jax, jax.numpy as jnp
from jax import lax
import numpy as np

D_MODEL = 2048
BATCH = 4
SEQ = 4096
DEPTH = 4

N_MIXERS = 3
N_HEADS = 16
HEAD_DIM = D_MODEL // N_HEADS
Q_BLOCK = 128
POOL_WINDOWS = (2, 4, 8, 16)
N_POOL_GROUPS = len(POOL_WINDOWS)
POOL_GROUP = D_MODEL // N_POOL_GROUPS
CONV_W = 3
D_FF = ((8 * D_MODEL // 3 + 255) // 256) * 256
EPS = 1e-6
N_SB = (DEPTH + 2) // 3
N_POOL = (DEPTH + 1) // 3
N_CONV = DEPTH // 3

kernel_name = "hybrid_stickbreak_pool_shortconv_trunk"


def rmsnorm(x, g):
    xf = x.astype(jnp.float32)
    y = xf * lax.rsqrt(jnp.mean(xf * xf, axis=-1, keepdims=True) + EPS)
    return (y * g.astype(jnp.float32)).astype(x.dtype)


def head_rmsnorm(x, g):
    xf = x.astype(jnp.float32)
    return xf * lax.rsqrt(jnp.mean(xf * xf, axis=-1, keepdims=True) + EPS) * g.astype(jnp.float32)


def stick_breaking_attention(h, w_qkv, g_q, g_k, w_o):
    B, S, D = h.shape
    qkv = h @ w_qkv
    q, k, v = jnp.split(qkv, 3, axis=-1)
    q = head_rmsnorm(q.reshape(B, S, N_HEADS, HEAD_DIM), g_q)
    k = head_rmsnorm(k.reshape(B, S, N_HEADS, HEAD_DIM), g_k)
    v = v.reshape(B, S, N_HEADS, HEAD_DIM).astype(jnp.float32)
    scale = HEAD_DIM ** -0.5
    outs = []
    for blk in range(S // Q_BLOCK):
        t0 = blk * Q_BLOCK
        kl = t0 + Q_BLOCK
        z = jnp.einsum('bthd,bshd->bhts', q[:, t0:kl], k[:, :kl]) * scale
        t_idx = t0 + jnp.arange(Q_BLOCK)[:, None]
        s_idx = jnp.arange(kl)[None, :]
        mask = s_idx < t_idx
        log_keep = jnp.where(mask, jax.nn.log_sigmoid(-z), 0.0)
        after = lax.cumsum(log_keep, axis=3, reverse=True) - log_keep
        log_a = jax.nn.log_sigmoid(z) + after
        a = jnp.where(mask, jnp.exp(log_a), 0.0)
        o = jnp.einsum('bhts,bshd->bthd', a, v[:, :kl])
        outs.append(o.astype(h.dtype))
    o = jnp.concatenate(outs, axis=1).reshape(B, S, D)
    return o @ w_o


def multiscale_pool(h, w_pool, scale):
    B, S, D = h.shape
    hg = h.astype(jnp.float32).reshape(B, S, N_POOL_GROUPS, POOL_GROUP)
    c = jnp.cumsum(hg, axis=1)
    pos1 = jnp.arange(1, S + 1)
    pieces = []
    for g, w in enumerate(POOL_WINDOWS):
        cg = c[:, :, g]
        lag = jnp.pad(cg, ((0, 0), (w, 0), (0, 0)))[:, :S]
        cnt = jnp.minimum(pos1, w).astype(jnp.float32)[None, :, None]
        pieces.append((cg - lag) / cnt - hg[:, :, g])
    p = jnp.stack(pieces, axis=2).astype(h.dtype)
    y = jnp.einsum('bsgc,gcd->bsgd', p, w_pool).reshape(B, S, D)
    return y * scale


def short_gated_conv(h, w_in, w_conv, w_out):
    D = h.shape[-1]
    bcx = h @ w_in
    b, c, u = jnp.split(bcx, 3, axis=-1)
    g = c * u
    y = lax.conv_general_dilated(
        g, w_conv[:, None, :].astype(g.dtype), window_strides=(1,),
        padding=[(CONV_W - 1, 0)], dimension_numbers=('NWC', 'WIO', 'NWC'),
        feature_group_count=D)
    return (b * y) @ w_out


def swiglu(h, w_gate, w_up, w_down):
    return (jax.nn.silu(h @ w_gate) * (h @ w_up)) @ w_down


def setup_inputs(seed: int = 0) -> dict:
    key = jax.random.key(seed)
    ks = jax.random.split(key, 16)
    f32 = jnp.float32
    D, F = D_MODEL, D_FF
    def nrm(k, shape, s):
        return jax.random.normal(k, shape, f32) * s
    return {
        "x": jax.random.normal(ks[0], (BATCH, SEQ, D), f32),
        "norm_mix_g": 1.0 + nrm(ks[1], (DEPTH, D), 0.02),
        "norm_ffn_g": 1.0 + nrm(ks[2], (DEPTH, D), 0.02),
        "sb_w_qkv": nrm(ks[3], (N_SB, D, 3 * D), D ** -0.5),
        "sb_g_q": 1.0 + nrm(ks[4], (N_SB, HEAD_DIM), 0.02),
        "sb_g_k": 1.0 + nrm(ks[5], (N_SB, HEAD_DIM), 0.02),
        "sb_w_o": nrm(ks[6], (N_SB, D, D), D ** -0.5),
        "pool_w": nrm(ks[7], (N_POOL, N_POOL_GROUPS, POOL_GROUP, POOL_GROUP), POOL_GROUP ** -0.5),
        "pool_scale": 1.0 + nrm(ks[8], (N_POOL, D), 0.02),
        "conv_w_in": nrm(ks[9], (N_CONV, D, 3 * D), D ** -0.5),
        "conv_w": nrm(ks[10], (N_CONV, CONV_W, D), CONV_W ** -0.5),
        "conv_w_out": nrm(ks[11], (N_CONV, D, D), D ** -0.5),
        "ffn_w_gate": nrm(ks[12], (DEPTH, D, F), D ** -0.5),
        "ffn_w_up": nrm(ks[13], (DEPTH, D, F), D ** -0.5),
        "ffn_w_down": nrm(ks[14], (DEPTH, F, D), F ** -0.5),
    }


def reference(x, norm_mix_g, norm_ffn_g, sb_w_qkv, sb_g_q, sb_g_k, sb_w_o,
              pool_w, pool_scale, conv_w_in, conv_w, conv_w_out,
              ffn_w_gate, ffn_w_up, ffn_w_down):
    for i in range(DEPTH):
        kind, j = i % N_MIXERS, i // N_MIXERS
        h = rmsnorm(x, norm_mix_g[i])
        if kind == 0:
            x = x + stick_breaking_attention(h, sb_w_qkv[j], sb_g_q[j], sb_g_k[j], sb_w_o[j])
        elif kind == 1:
            x = x + multiscale_pool(h, pool_w[j], pool_scale[j])
        else:
            x = x + short_gated_conv(h, conv_w_in[j], conv_w[j], conv_w_out[j])
        h = rmsnorm(x, norm_ffn_g[i])
        x = x + swiglu(h, ffn_w_gate[i], ffn_w_up[i], ffn_w_down[i])
    return x
```

```python
import functools

import jax
import jax.numpy as jnp
from jax import lax
from jax.experimental import pallas as pl
from jax.experimental.pallas import tpu as pltpu

F32 = jnp.float32
BF16 = jnp.bfloat16

EPS = 1e-6
N_HEADS = 16
POOL_WINDOWS = (2, 4, 8, 16)
CONV_W = 3
N_MIXERS = 3

LANES = 128
SUBLANES = 8
VMEM_LIMIT_BYTES = 56 * 1024 * 1024

TM_PROJ = 1024
TN_PROJ = 256
TM_RES = 512
TM_FFN = 512
TF_FFN = 512
TS_POOL = 512
TQ = 256
TK = 256
HALO = 16


def _params(*sem):
    return pltpu.CompilerParams(dimension_semantics=sem,
                                vmem_limit_bytes=VMEM_LIMIT_BYTES)


def _rmsnorm(x, g):
    ms = jnp.mean(x * x, axis=-1, keepdims=True)
    return (x * lax.rsqrt(ms + EPS)) * g


def _dot(a, b):
    return jnp.dot(a, b, preferred_element_type=F32)


def _qkv_kernel(x_ref, g_ref, wq_ref, wk_ref, wv_ref, gq_ref, gk_ref,
                q_ref, k_ref, v_ref, h_ref):
    @pl.when(pl.program_id(1) == 0)
    def _():
        h_ref[...] = _rmsnorm(x_ref[...], g_ref[...]).astype(BF16)

    h = h_ref[...]

    def head_norm(y, gh, out_ref):
        for c in range(y.shape[1] // LANES):
            yc = y[:, c * LANES:(c + 1) * LANES]
            ms = jnp.mean(yc * yc, axis=-1, keepdims=True)
            out_ref[:, c * LANES:(c + 1) * LANES] = (
                yc * lax.rsqrt(ms + EPS) * gh).astype(BF16)

    head_norm(_dot(h, wq_ref[...]), gq_ref[...], q_ref)
    head_norm(_dot(h, wk_ref[...]), gk_ref[...], k_ref)
    v_ref[...] = _dot(h, wv_ref[...]).astype(BF16)


def _qkv_proj(x, g, w3, layer, gq, gk):
    m, d = x.shape
    nj = d // TN_PROJ
    out = jax.ShapeDtypeStruct((m, d), BF16)

    def wspec(part):
        return pl.BlockSpec((None, d, TN_PROJ),
                            lambda i, j: (layer, 0, part * nj + j))

    ospec = pl.BlockSpec((TM_PROJ, TN_PROJ), lambda i, j: (i, j))
    return pl.pallas_call(
        _qkv_kernel,
        grid=(m // TM_PROJ, nj),
        in_specs=[
            pl.BlockSpec((TM_PROJ, d), lambda i, j: (i, 0)),
            pl.BlockSpec((1, d), lambda i, j: (0, 0)),
            wspec(0), wspec(1), wspec(2),
            pl.BlockSpec((1, LANES), lambda i, j: (0, 0)),
            pl.BlockSpec((1, LANES), lambda i, j: (0, 0)),
        ],
        out_specs=[ospec, ospec, ospec],
        out_shape=[out, out, out],
        scratch_shapes=[pltpu.VMEM((TM_PROJ, d), BF16)],
        compiler_params=_params("arbitrary", "arbitrary"),
        name="qkv_proj",
    )(x, g, w3, w3, w3, gq, gk)


def _sb_kernel(q_ref, k_ref, v_ref, uj_ref, o_ref, acc_ref, r_ref):
    i = pl.program_id(2)
    q = q_ref[...]
    uj = uj_ref[...]
    scale = q.shape[1] ** -0.5

    def block(kb, masked):
        k0 = pl.multiple_of(kb * TK, TK)
        kblk = k_ref[pl.ds(k0, TK), :]
        vblk = v_ref[pl.ds(k0, TK), :]
        z = lax.dot_general(q, kblk, (((1,), (1,)), ((), ())),
                            preferred_element_type=F32) * scale
        r = r_ref[...]
        parts = [None] * (TK // LANES)
        for sb in reversed(range(TK // LANES)):
            zs = z[:, sb * LANES:(sb + 1) * LANES]
            lse = jnp.log(1.0 + jnp.exp(-jnp.abs(zs)))
            log_beta = jnp.minimum(zs, 0.0) - lse
            log_keep = log_beta - zs
            if masked:
                t_idx = lax.broadcasted_iota(jnp.int32, zs.shape, 0)
                s_idx = lax.broadcasted_iota(jnp.int32, zs.shape, 1) + sb * LANES
                mask = s_idx < t_idx
                log_keep = jnp.where(mask, log_keep, 0.0)
            hi = log_keep.astype(BF16)
            lo = (log_keep - hi.astype(F32)).astype(BF16)
            ct = _dot(jnp.concatenate([hi, lo], axis=1), uj)
            a = jnp.exp(log_beta + (r + ct[:, :LANES]))
            if masked:
                a = jnp.where(mask, a, 0.0)
            parts[sb] = a.astype(BF16)
            r = r + ct[:, LANES:]
        r_ref[...] = r
        acc_ref[...] += _dot(jnp.concatenate(parts, axis=1), vblk)

    r_ref[...] = jnp.zeros_like(r_ref)
    acc_ref[...] = jnp.zeros_like(acc_ref)
    block(i, True)

    def body(step, carry):
        block(i - 1 - step, False)
        return carry

    lax.fori_loop(0, i, body, 0)
    o_ref[...] = acc_ref[...].astype(BF16)


def _suffix_sum_matrix():
    j = lax.broadcasted_iota(jnp.int32, (2 * LANES, 2 * LANES), 0) % LANES
    s = lax.broadcasted_iota(jnp.int32, (2 * LANES, 2 * LANES), 1)
    return jnp.where((s >= LANES) | (j > s), 1.0, 0.0).astype(BF16)


def _sb_attention(q, k, v, batch, seq):
    m, d = q.shape
    dh = d // N_HEADS
    q3, k3, v3 = (t.reshape(batch, seq, d) for t in (q, k, v))
    kv_spec = pl.BlockSpec((None, seq, dh), lambda b, h, i: (b, 0, h))
    qo_spec = pl.BlockSpec((None, TQ, dh), lambda b, h, i: (b, i, h))
    o = pl.pallas_call(
        _sb_kernel,
        grid=(batch, N_HEADS, seq // TQ),
        in_specs=[qo_spec, kv_spec, kv_spec,
                  pl.BlockSpec((2 * LANES, 2 * LANES), lambda b, h, i: (0, 0))],
        out_specs=qo_spec,
        out_shape=jax.ShapeDtypeStruct((batch, seq, d), BF16),
        scratch_shapes=[pltpu.VMEM((TQ, dh), F32), pltpu.VMEM((TQ, LANES), F32)],
        compiler_params=_params("arbitrary", "arbitrary", "arbitrary"),
        name="sb_attention",
    )(q3, k3, v3, _suffix_sum_matrix())
    return o.reshape(m, d)


def _proj_res_kernel(a_ref, w_ref, x_ref, o_ref):
    o_ref[...] = x_ref[...] + _dot(a_ref[...], w_ref[...])


def _proj_residual(a, w, layer, x):
    m, d = x.shape
    return pl.pallas_call(
        _proj_res_kernel,
        grid=(m // TM_RES,),
        in_specs=[
            pl.BlockSpec((TM_RES, d), lambda i: (i, 0)),
            pl.BlockSpec((None, d, d), lambda i: (layer, 0, 0)),
            pl.BlockSpec((TM_RES, d), lambda i: (i, 0)),
        ],
        out_specs=pl.BlockSpec((TM_RES, d), lambda i: (i, 0)),
        out_shape=jax.ShapeDtypeStruct((m, d), F32),
        compiler_params=_params("arbitrary"),
        name="proj_residual",
    )(a, w, x)


def _pool_kernel(x_ref, g_ref, w_ref, sc_ref, o_ref, buf_ref):
    j = pl.program_id(1)
    ts, d = x_ref.shape
    gw = d // len(POOL_WINDOWS)

    @pl.when(j == 0)
    def _():
        buf_ref[0:HALO, :] = jnp.zeros((HALO, d), F32)

    @pl.when(j > 0)
    def _():
        buf_ref[0:HALO, :] = buf_ref[ts:ts + HALO, :]

    x = x_ref[...]
    buf_ref[HALO:HALO + ts, :] = _rmsnorm(x, g_ref[...])
    pos1 = j * ts + 1 + lax.broadcasted_iota(jnp.int32, (ts, 1), 0)
    for gi, win in enumerate(POOL_WINDOWS):
        cols = slice(gi * gw, (gi + 1) * gw)
        cur = buf_ref[HALO:HALO + ts, cols]
        tot = cur
        for lag in range(1, win):
            tot = tot + buf_ref[HALO - lag:HALO - lag + ts, cols]
        cnt = jnp.minimum(pos1, win).astype(F32)
        piece = (tot / cnt - cur).astype(BF16)
        y = _dot(piece, w_ref[gi])
        o_ref[:, cols] = x[:, cols] + y * sc_ref[:, cols]


def _pool_mixer(x, g, w, layer, sc, batch, seq):
    m, d = x.shape
    ng = len(POOL_WINDOWS)
    gw = d // ng
    nt = seq // TS_POOL
    xspec = pl.BlockSpec((TS_POOL, d), lambda b, j: (b * nt + j, 0))
    vspec = pl.BlockSpec((1, d), lambda b, j: (0, 0))
    return pl.pallas_call(
        _pool_kernel,
        grid=(batch, nt),
        in_specs=[xspec, vspec,
                  pl.BlockSpec((None, ng, gw, gw), lambda b, j: (layer, 0, 0, 0)),
                  vspec],
        out_specs=xspec,
        out_shape=jax.ShapeDtypeStruct((m, d), F32),
        scratch_shapes=[pltpu.VMEM((HALO + TS_POOL, d), F32)],
        compiler_params=_params("arbitrary", "arbitrary"),
        name="pool_mixer",
    )(x, g, w, sc)


def _conv_in_kernel(tiles_per_seq, x_ref, g_ref, wb_ref, wc_ref, wu_ref, cw_ref,
                    o_ref, h_ref, carry_ref):
    i = pl.program_id(0)
    j = pl.program_id(1)

    @pl.when(j == 0)
    def _():
        h_ref[...] = _rmsnorm(x_ref[...], g_ref[...]).astype(BF16)

    @pl.when(i % tiles_per_seq == 0)
    def _():
        carry_ref[j] = jnp.zeros(carry_ref.shape[1:], F32)

    h = h_ref[...]
    gate = _dot(h, wb_ref[...])
    cu = _dot(h, wc_ref[...]) * _dot(h, wu_ref[...])
    tm = cu.shape[0]
    prev = carry_ref[j]
    p1 = prev[SUBLANES - 1:SUBLANES, :]
    p2 = prev[SUBLANES - 2:SUBLANES - 1, :]
    rows = lax.broadcasted_iota(jnp.int32, cu.shape, 0)
    lag1 = jnp.where(rows == 0, p1, pltpu.roll(cu, 1, 0))
    lag2 = jnp.where(rows == 0, p2, jnp.where(rows == 1, p1, pltpu.roll(cu, 2, 0)))
    cw = cw_ref[...]
    y = cw[0:1, :] * lag2 + cw[1:2, :] * lag1 + cw[2:3, :] * cu
    carry_ref[j] = cu[tm - SUBLANES:tm, :]
    o_ref[...] = (gate * y).astype(BF16)


def _conv_in(x, g, w3, layer, cw, seq):
    m, d = x.shape
    nj = d // TN_PROJ

    def wspec(part):
        return pl.BlockSpec((None, d, TN_PROJ),
                            lambda i, j: (layer, 0, part * nj + j))

    return pl.pallas_call(
        functools.partial(_conv_in_kernel, seq // TM_PROJ),
        grid=(m // TM_PROJ, nj),
        in_specs=[
            pl.BlockSpec((TM_PROJ, d), lambda i, j: (i, 0)),
            pl.BlockSpec((1, d), lambda i, j: (0, 0)),
            wspec(0), wspec(1), wspec(2),
            pl.BlockSpec((None, CONV_W, TN_PROJ), lambda i, j: (layer, 0, j)),
        ],
        out_specs=pl.BlockSpec((TM_PROJ, TN_PROJ), lambda i, j: (i, j)),
        out_shape=jax.ShapeDtypeStruct((m, d), BF16),
        scratch_shapes=[pltpu.VMEM((TM_PROJ, d), BF16),
                        pltpu.VMEM((nj, SUBLANES, TN_PROJ), F32)],
        compiler_params=_params("arbitrary", "arbitrary"),
        name="conv_in",
    )(x, g, w3, w3, w3, cw)


def _ffn_kernel(x_ref, g_ref, wg_ref, wu_ref, wd_ref, o_ref, h_ref):
    @pl.when(pl.program_id(1) == 0)
    def _():
        x = x_ref[...]
        h_ref[...] = _rmsnorm(x, g_ref[...]).astype(BF16)
        o_ref[...] = x

    h = h_ref[...]
    gate = _dot(h, wg_ref[...])
    up = _dot(h, wu_ref[...])
    act = (gate * jax.nn.sigmoid(gate) * up).astype(BF16)
    o_ref[...] += _dot(act, wd_ref[...])


def _ffn(x, g, wg, wu, wd, layer):
    m, d = x.shape
    f = wg.shape[2]
    return pl.pallas_call(
        _ffn_kernel,
        grid=(m // TM_FFN, f // TF_FFN),
        in_specs=[
            pl.BlockSpec((TM_FFN, d), lambda i, j: (i, 0)),
            pl.BlockSpec((1, d), lambda i, j: (0, 0)),
            pl.BlockSpec((None, d, TF_FFN), lambda i, j: (layer, 0, j)),
            pl.BlockSpec((None, d, TF_FFN), lambda i, j: (layer, 0, j)),
            pl.BlockSpec((None, TF_FFN, d), lambda i, j: (layer, j, 0)),
        ],
        out_specs=pl.BlockSpec((TM_FFN, d), lambda i, j: (i, 0)),
        out_shape=jax.ShapeDtypeStruct((m, d), F32),
        scratch_shapes=[pltpu.VMEM((TM_FFN, d), BF16)],
        compiler_params=_params("arbitrary", "arbitrary"),
        name="ffn",
    )(x, g, wg, wu, wd)


def kernel(x, norm_mix_g, norm_ffn_g, sb_w_qkv, sb_g_q, sb_g_k, sb_w_o, pool_w, pool_scale, conv_w_in, conv_w, conv_w_out, ffn_w_gate, ffn_w_up, ffn_w_down):
    batch, seq, d = x.shape
    depth = norm_mix_g.shape[0]
    xf = x.reshape(batch * seq, d)
    (sb_w_qkv, sb_w_o, pool_w, conv_w_in, conv_w_out, ffn_w_gate, ffn_w_up,
     ffn_w_down) = (w.astype(BF16) for w in (
         sb_w_qkv, sb_w_o, pool_w, conv_w_in, conv_w_out, ffn_w_gate, ffn_w_up,
         ffn_w_down))
    for i in range(depth):
        kind, j = i % N_MIXERS, i // N_MIXERS
        g_mix = norm_mix_g[i:i + 1]
        if kind == 0:
            q, k, v = _qkv_proj(xf, g_mix, sb_w_qkv, j, sb_g_q[j:j + 1], sb_g_k[j:j + 1])
            o = _sb_attention(q, k, v, batch, seq)
            xf = _proj_residual(o, sb_w_o, j, xf)
        elif kind == 1:
            xf = _pool_mixer(xf, g_mix, pool_w, j, pool_scale[j:j + 1], batch, seq)
        else:
            gated = _conv_in(xf, g_mix, conv_w_in, j, conv_w, seq)
            xf = _proj_residual(gated, conv_w_out, j, xf)
        xf = _ffn(xf, norm_ffn_g[i:i + 1], ffn_w_gate, ffn_w_up, ffn_w_down, i)
    return xf.reshape(batch, seq, d)
```

```python
import functools

import jax
import jax.numpy as jnp
from jax import lax
from jax.experimental import pallas as pl
from jax.experimental.pallas import tpu as pltpu

F32 = jnp.float32
BF16 = jnp.bfloat16

EPS = 1e-6
LOG2E = 1.4426950408889634
N_HEADS = 16
POOL_WINDOWS = (2, 4, 8, 16)
CONV_W = 3
N_MIXERS = 3

LANES = 128
SUBLANES = 8
VMEM_LIMIT_BYTES = 56 * 1024 * 1024

TM_PROJ = 1024
TN_PROJ = 256
TM_RES = 512
TM_FFN = 1024
TF_FFN = 512
TS_POOL = 512
TQ = 512
TD = 256
HALO = 16


def _params(*sem):
    return pltpu.CompilerParams(dimension_semantics=sem,
                                vmem_limit_bytes=VMEM_LIMIT_BYTES)


def _rmsnorm(x, g):
    ms = jnp.mean(x * x, axis=-1, keepdims=True)
    return (x * lax.rsqrt(ms + EPS)) * g


def _dot(a, b):
    return jnp.dot(a, b, preferred_element_type=F32)


def _qkv_kernel(x_ref, g_ref, wq_ref, wk_ref, wv_ref, gq_ref, gk_ref,
                q_ref, k_ref, v_ref, h_ref):
    @pl.when(pl.program_id(1) == 0)
    def _():
        h_ref[...] = _rmsnorm(x_ref[...], g_ref[...]).astype(BF16)

    h = h_ref[...]

    def head_norm(y, gh, out_ref):
        for c in range(y.shape[1] // LANES):
            yc = y[:, c * LANES:(c + 1) * LANES]
            ms = jnp.mean(yc * yc, axis=-1, keepdims=True)
            out_ref[:, c * LANES:(c + 1) * LANES] = (
                yc * lax.rsqrt(ms + EPS) * gh).astype(BF16)

    head_norm(_dot(h, wq_ref[...]), gq_ref[...], q_ref)
    head_norm(_dot(h, wk_ref[...]), gk_ref[...], k_ref)
    v_ref[...] = _dot(h, wv_ref[...]).astype(BF16)


def _qkv_proj(x, g, w3, layer, gq, gk):
    m, d = x.shape
    nj = d // TN_PROJ
    out = jax.ShapeDtypeStruct((m, d), BF16)

    def wspec(part):
        return pl.BlockSpec((None, d, TN_PROJ),
                            lambda i, j: (layer, 0, part * nj + j))

    ospec = pl.BlockSpec((TM_PROJ, TN_PROJ), lambda i, j: (i, j))
    return pl.pallas_call(
        _qkv_kernel,
        grid=(m // TM_PROJ, nj),
        in_specs=[
            pl.BlockSpec((TM_PROJ, d), lambda i, j: (i, 0)),
            pl.BlockSpec((1, d), lambda i, j: (0, 0)),
            wspec(0), wspec(1), wspec(2),
            pl.BlockSpec((1, LANES), lambda i, j: (0, 0)),
            pl.BlockSpec((1, LANES), lambda i, j: (0, 0)),
        ],
        out_specs=[ospec, ospec, ospec],
        out_shape=[out, out, out],
        scratch_shapes=[pltpu.VMEM((TM_PROJ, d), BF16)],
        compiler_params=_params("arbitrary", "arbitrary"),
        name="qkv_proj",
    )(x, g, w3, w3, w3, gq, gk)


def _sb_tile(q_ref, k_ref, v_ref, uj, r_ref, acc_ref, row0, nrows, k0, nkeys, masked):
    rows = slice(row0, row0 + nrows)
    q = q_ref[rows, :]
    kblk = k_ref[pl.ds(k0, nkeys), :]
    vblk = v_ref[pl.ds(k0, nkeys), :]
    w = lax.dot_general(q, kblk, (((1,), (1,)), ((), ())),
                        preferred_element_type=F32) * (q.shape[1] ** -0.5 * LOG2E)
    r = r_ref[rows, :]
    parts = [None] * (nkeys // LANES)
    for sb in reversed(range(nkeys // LANES)):
        ws = w[:, sb * LANES:(sb + 1) * LANES]
        lse = jnp.log2(1.0 + jnp.exp2(-jnp.abs(ws)))
        log_beta = jnp.minimum(ws, 0.0) - lse
        log_keep = log_beta - ws
        if masked:
            t_idx = lax.broadcasted_iota(jnp.int32, ws.shape, 0)
            s_idx = lax.broadcasted_iota(jnp.int32, ws.shape, 1) + sb * LANES
            mask = s_idx < t_idx
            log_keep = jnp.where(mask, log_keep, 0.0)
        hi = log_keep.astype(BF16)
        lo = (log_keep - hi.astype(F32)).astype(BF16)
        ct = _dot(jnp.concatenate([hi, lo], axis=1), uj)
        a = jnp.exp2(log_beta + (r + ct[:, :LANES]))
        if masked:
            a = jnp.where(mask, a, 0.0)
        parts[sb] = a.astype(BF16)
        r = r + ct[:, LANES:]
    r_ref[rows, :] = r
    acc_ref[rows, :] += _dot(jnp.concatenate(parts, axis=1), vblk)


def _sb_kernel(q_ref, k_ref, v_ref, uj_ref, o_ref, acc_ref, r_ref):
    i = pl.program_id(2)
    uj = uj_ref[...]
    tile = functools.partial(_sb_tile, q_ref, k_ref, v_ref, uj, r_ref, acc_ref)

    r_ref[...] = jnp.zeros_like(r_ref)
    acc_ref[...] = jnp.zeros_like(acc_ref)
    d0 = pl.multiple_of(i * TQ, TQ)
    for rt in reversed(range(TQ // TD)):
        tile(rt * TD, TD, d0 + rt * TD, TD, True)
        for kt in reversed(range(rt)):
            tile(rt * TD, TD, d0 + kt * TD, TD, False)

    def off_diagonal(kb):
        tile(0, TQ, pl.multiple_of(kb * TQ, TQ), TQ, False)

    @pl.when(i % 2 == 1)
    def _():
        off_diagonal(i - 1)

    def body(step, carry):
        kb = (i // 2 - step) * 2 - 1
        off_diagonal(kb)
        off_diagonal(kb - 1)
        return carry

    lax.fori_loop(0, i // 2, body, 0)
    o_ref[...] = acc_ref[...].astype(BF16)


def _suffix_sum_matrix():
    j = lax.broadcasted_iota(jnp.int32, (2 * LANES, 2 * LANES), 0) % LANES
    s = lax.broadcasted_iota(jnp.int32, (2 * LANES, 2 * LANES), 1)
    return jnp.where((s >= LANES) | (j > s), 1.0, 0.0).astype(BF16)


def _sb_attention(q, k, v, batch, seq):
    m, d = q.shape
    dh = d // N_HEADS
    q3, k3, v3 = (t.reshape(batch, seq, d) for t in (q, k, v))
    kv_spec = pl.BlockSpec((None, seq, dh), lambda b, h, i: (b, 0, h))
    qo_spec = pl.BlockSpec((None, TQ, dh), lambda b, h, i: (b, i, h))
    o = pl.pallas_call(
        _sb_kernel,
        grid=(batch, N_HEADS, seq // TQ),
        in_specs=[qo_spec, kv_spec, kv_spec,
                  pl.BlockSpec((2 * LANES, 2 * LANES), lambda b, h, i: (0, 0))],
        out_specs=qo_spec,
        out_shape=jax.ShapeDtypeStruct((batch, seq, d), BF16),
        scratch_shapes=[pltpu.VMEM((TQ, dh), F32), pltpu.VMEM((TQ, LANES), F32)],
        compiler_params=_params("arbitrary", "arbitrary", "arbitrary"),
        name="sb_attention",
    )(q3, k3, v3, _suffix_sum_matrix())
    return o.reshape(m, d)


def _proj_res_kernel(a_ref, w_ref, x_ref, o_ref):
    o_ref[...] = x_ref[...] + _dot(a_ref[...], w_ref[...])


def _proj_residual(a, w, layer, x):
    m, d = x.shape
    return pl.pallas_call(
        _proj_res_kernel,
        grid=(m // TM_RES,),
        in_specs=[
            pl.BlockSpec((TM_RES, d), lambda i: (i, 0)),
            pl.BlockSpec((None, d, d), lambda i: (layer, 0, 0)),
            pl.BlockSpec((TM_RES, d), lambda i: (i, 0)),
        ],
        out_specs=pl.BlockSpec((TM_RES, d), lambda i: (i, 0)),
        out_shape=jax.ShapeDtypeStruct((m, d), F32),
        compiler_params=_params("arbitrary"),
        name="proj_residual",
    )(a, w, x)


def _pool_kernel(x_ref, g_ref, w_ref, sc_ref, o_ref, buf_ref):
    j = pl.program_id(1)
    ts, d = x_ref.shape
    gw = d // len(POOL_WINDOWS)

    @pl.when(j == 0)
    def _():
        buf_ref[0:HALO, :] = jnp.zeros((HALO, d), F32)

    @pl.when(j > 0)
    def _():
        buf_ref[0:HALO, :] = buf_ref[ts:ts + HALO, :]

    x = x_ref[...]
    buf_ref[HALO:HALO + ts, :] = _rmsnorm(x, g_ref[...])
    pos1 = j * ts + 1 + lax.broadcasted_iota(jnp.int32, (ts, 1), 0)
    for gi, win in enumerate(POOL_WINDOWS):
        cols = slice(gi * gw, (gi + 1) * gw)
        cur = buf_ref[HALO:HALO + ts, cols]
        tot = cur
        for lag in range(1, win):
            tot = tot + buf_ref[HALO - lag:HALO - lag + ts, cols]
        cnt = jnp.minimum(pos1, win).astype(F32)
        piece = (tot / cnt - cur).astype(BF16)
        y = _dot(piece, w_ref[gi])
        o_ref[:, cols] = x[:, cols] + y * sc_ref[:, cols]


def _pool_mixer(x, g, w, layer, sc, batch, seq):
    m, d = x.shape
    ng = len(POOL_WINDOWS)
    gw = d // ng
    nt = seq // TS_POOL
    xspec = pl.BlockSpec((TS_POOL, d), lambda b, j: (b * nt + j, 0))
    vspec = pl.BlockSpec((1, d), lambda b, j: (0, 0))
    return pl.pallas_call(
        _pool_kernel,
        grid=(batch, nt),
        in_specs=[xspec, vspec,
                  pl.BlockSpec((None, ng, gw, gw), lambda b, j: (layer, 0, 0, 0)),
                  vspec],
        out_specs=xspec,
        out_shape=jax.ShapeDtypeStruct((m, d), F32),
        scratch_shapes=[pltpu.VMEM((HALO + TS_POOL, d), F32)],
        compiler_params=_params("arbitrary", "arbitrary"),
        name="pool_mixer",
    )(x, g, w, sc)


def _conv_in_kernel(tiles_per_seq, x_ref, g_ref, wb_ref, wc_ref, wu_ref, cw_ref,
                    o_ref, h_ref, carry_ref):
    i = pl.program_id(0)
    j = pl.program_id(1)

    @pl.when(j == 0)
    def _():
        h_ref[...] = _rmsnorm(x_ref[...], g_ref[...]).astype(BF16)

    @pl.when(i % tiles_per_seq == 0)
    def _():
        carry_ref[j] = jnp.zeros(carry_ref.shape[1:], F32)

    h = h_ref[...]
    gate = _dot(h, wb_ref[...])
    cu = _dot(h, wc_ref[...]) * _dot(h, wu_ref[...])
    tm = cu.shape[0]
    prev = carry_ref[j]
    p1 = prev[SUBLANES - 1:SUBLANES, :]
    p2 = prev[SUBLANES - 2:SUBLANES - 1, :]
    rows = lax.broadcasted_iota(jnp.int32, cu.shape, 0)
    lag1 = jnp.where(rows == 0, p1, pltpu.roll(cu, 1, 0))
    lag2 = jnp.where(rows == 0, p2, jnp.where(rows == 1, p1, pltpu.roll(cu, 2, 0)))
    cw = cw_ref[...]
    y = cw[0:1, :] * lag2 + cw[1:2, :] * lag1 + cw[2:3, :] * cu
    carry_ref[j] = cu[tm - SUBLANES:tm, :]
    o_ref[...] = (gate * y).astype(BF16)


def _conv_in(x, g, w3, layer, cw, seq):
    m, d = x.shape
    nj = d // TN_PROJ

    def wspec(part):
        return pl.BlockSpec((None, d, TN_PROJ),
                            lambda i, j: (layer, 0, part * nj + j))

    return pl.pallas_call(
        functools.partial(_conv_in_kernel, seq // TM_PROJ),
        grid=(m // TM_PROJ, nj),
        in_specs=[
            pl.BlockSpec((TM_PROJ, d), lambda i, j: (i, 0)),
            pl.BlockSpec((1, d), lambda i, j: (0, 0)),
            wspec(0), wspec(1), wspec(2),
            pl.BlockSpec((None, CONV_W, TN_PROJ), lambda i, j: (layer, 0, j)),
        ],
        out_specs=pl.BlockSpec((TM_PROJ, TN_PROJ), lambda i, j: (i, j)),
        out_shape=jax.ShapeDtypeStruct((m, d), BF16),
        scratch_shapes=[pltpu.VMEM((TM_PROJ, d), BF16),
                        pltpu.VMEM((nj, SUBLANES, TN_PROJ), F32)],
        compiler_params=_params("arbitrary", "arbitrary"),
        name="conv_in",
    )(x, g, w3, w3, w3, cw)


def _ffn_kernel(x_ref, g_ref, wg_ref, wu_ref, wd_ref, o_ref, h_ref):
    @pl.when(pl.program_id(1) == 0)
    def _():
        x = x_ref[...]
        h_ref[...] = _rmsnorm(x, g_ref[...]).astype(BF16)
        o_ref[...] = x

    h = h_ref[...]
    gate = _dot(h, wg_ref[...])
    up = _dot(h, wu_ref[...])
    act = (gate * jax.nn.sigmoid(gate) * up).astype(BF16)
    o_ref[...] += _dot(act, wd_ref[...])


def _ffn(x, g, wg, wu, wd, layer):
    m, d = x.shape
    f = wg.shape[2]
    return pl.pallas_call(
        _ffn_kernel,
        grid=(m // TM_FFN, f // TF_FFN),
        in_specs=[
            pl.BlockSpec((TM_FFN, d), lambda i, j: (i, 0)),
            pl.BlockSpec((1, d), lambda i, j: (0, 0)),
            pl.BlockSpec((None, d, TF_FFN), lambda i, j: (layer, 0, j)),
            pl.BlockSpec((None, d, TF_FFN), lambda i, j: (layer, 0, j)),
            pl.BlockSpec((None, TF_FFN, d), lambda i, j: (layer, j, 0)),
        ],
        out_specs=pl.BlockSpec((TM_FFN, d), lambda i, j: (i, 0)),
        out_shape=jax.ShapeDtypeStruct((m, d), F32),
        scratch_shapes=[pltpu.VMEM((TM_FFN, d), BF16)],
        compiler_params=_params("arbitrary", "arbitrary"),
        name="ffn",
    )(x, g, wg, wu, wd)


def kernel(x, norm_mix_g, norm_ffn_g, sb_w_qkv, sb_g_q, sb_g_k, sb_w_o, pool_w, pool_scale, conv_w_in, conv_w, conv_w_out, ffn_w_gate, ffn_w_up, ffn_w_down):
    batch, seq, d = x.shape
    depth = norm_mix_g.shape[0]
    xf = x.reshape(batch * seq, d)
    (sb_w_qkv, sb_w_o, pool_w, conv_w_in, conv_w_out, ffn_w_gate, ffn_w_up,
     ffn_w_down) = (w.astype(BF16) for w in (
         sb_w_qkv, sb_w_o, pool_w, conv_w_in, conv_w_out, ffn_w_gate, ffn_w_up,
         ffn_w_down))
    for i in range(depth):
        kind, j = i % N_MIXERS, i // N_MIXERS
        g_mix = norm_mix_g[i:i + 1]
        if kind == 0:
            q, k, v = _qkv_proj(xf, g_mix, sb_w_qkv, j, sb_g_q[j:j + 1], sb_g_k[j:j + 1])
            o = _sb_attention(q, k, v, batch, seq)
            xf = _proj_residual(o, sb_w_o, j, xf)
        elif kind == 1:
            xf = _pool_mixer(xf, g_mix, pool_w, j, pool_scale[j:j + 1], batch, seq)
        else:
            gated = _conv_in(xf, g_mix, conv_w_in, j, conv_w, seq)
            xf = _proj_residual(gated, conv_w_out, j, xf)
        xf = _ffn(xf, norm_ffn_g[i:i + 1], ffn_w_gate, ffn_w_up, ffn_w_down, i)
    return xf.reshape(batch, seq, d)
```

```python
import functools

import jax
import jax.numpy as jnp
from jax import lax
from jax.experimental import pallas as pl
from jax.experimental.pallas import tpu as pltpu

F32 = jnp.float32
BF16 = jnp.bfloat16

EPS = 1e-6
LOG2E = 1.4426950408889634
DEAD_STICK_LOG2 = -160.0
N_HEADS = 16
POOL_WINDOWS = (2, 4, 8, 16)
CONV_W = 3
N_MIXERS = 3

LANES = 128
SUBLANES = 8
VMEM_LIMIT_BYTES = 56 * 1024 * 1024

TM_PROJ = 1024
TN_PROJ = 256
TM_RES = 512
TM_FFN = 1024
TF_FFN = 512
TS_POOL = 512
TQ = 512
TD = 256
HALO = 16


def _params(*sem):
    return pltpu.CompilerParams(dimension_semantics=sem,
                                vmem_limit_bytes=VMEM_LIMIT_BYTES)


def _rmsnorm(x, g):
    ms = jnp.mean(x * x, axis=-1, keepdims=True)
    return (x * lax.rsqrt(ms + EPS)) * g


def _dot(a, b):
    return jnp.dot(a, b, preferred_element_type=F32)


def _qkv_kernel(x_ref, g_ref, wq_ref, wk_ref, wv_ref, gq_ref, gk_ref,
                q_ref, k_ref, v_ref, h_ref):
    @pl.when(pl.program_id(1) == 0)
    def _():
        h_ref[...] = _rmsnorm(x_ref[...], g_ref[...]).astype(BF16)

    h = h_ref[...]

    def head_norm(y, gh, out_ref):
        for c in range(y.shape[1] // LANES):
            yc = y[:, c * LANES:(c + 1) * LANES]
            ms = jnp.mean(yc * yc, axis=-1, keepdims=True)
            out_ref[:, c * LANES:(c + 1) * LANES] = (
                yc * lax.rsqrt(ms + EPS) * gh).astype(BF16)

    head_norm(_dot(h, wq_ref[...]), gq_ref[...], q_ref)
    head_norm(_dot(h, wk_ref[...]), gk_ref[...], k_ref)
    v_ref[...] = _dot(h, wv_ref[...]).astype(BF16)


def _qkv_proj(x, g, w3, layer, gq, gk):
    m, d = x.shape
    nj = d // TN_PROJ
    out = jax.ShapeDtypeStruct((m, d), BF16)

    def wspec(part):
        return pl.BlockSpec((None, d, TN_PROJ),
                            lambda i, j: (layer, 0, part * nj + j))

    ospec = pl.BlockSpec((TM_PROJ, TN_PROJ), lambda i, j: (i, j))
    return pl.pallas_call(
        _qkv_kernel,
        grid=(m // TM_PROJ, nj),
        in_specs=[
            pl.BlockSpec((TM_PROJ, d), lambda i, j: (i, 0)),
            pl.BlockSpec((1, d), lambda i, j: (0, 0)),
            wspec(0), wspec(1), wspec(2),
            pl.BlockSpec((1, LANES), lambda i, j: (0, 0)),
            pl.BlockSpec((1, LANES), lambda i, j: (0, 0)),
        ],
        out_specs=[ospec, ospec, ospec],
        out_shape=[out, out, out],
        scratch_shapes=[pltpu.VMEM((TM_PROJ, d), BF16)],
        compiler_params=_params("arbitrary", "arbitrary"),
        name="qkv_proj",
    )(x, g, w3, w3, w3, gq, gk)


def _sb_tile(q_ref, k_ref, v_ref, uj, r_ref, acc_ref, row0, nrows, k0, nkeys, masked):
    rows = slice(row0, row0 + nrows)
    q = q_ref[rows, :]
    kblk = k_ref[pl.ds(k0, nkeys), :]
    vblk = v_ref[pl.ds(k0, nkeys), :]
    w = lax.dot_general(q, kblk, (((1,), (1,)), ((), ())),
                        preferred_element_type=F32) * (q.shape[1] ** -0.5 * LOG2E)
    r = r_ref[rows, :]
    parts = [None] * (nkeys // LANES)
    for sb in reversed(range(nkeys // LANES)):
        ws = w[:, sb * LANES:(sb + 1) * LANES]
        lse = jnp.log2(1.0 + jnp.exp2(-jnp.abs(ws)))
        log_beta = jnp.minimum(ws, 0.0) - lse
        log_keep = log_beta - ws
        if masked:
            t_idx = lax.broadcasted_iota(jnp.int32, ws.shape, 0)
            s_idx = lax.broadcasted_iota(jnp.int32, ws.shape, 1) + sb * LANES
            mask = s_idx < t_idx
            log_keep = jnp.where(mask, log_keep, 0.0)
        hi = log_keep.astype(BF16)
        lo = (log_keep - hi.astype(F32)).astype(BF16)
        ct = _dot(jnp.concatenate([hi, lo], axis=1), uj)
        a = jnp.exp2(log_beta + (r + ct[:, :LANES]))
        if masked:
            a = jnp.where(mask, a, 0.0)
        parts[sb] = a.astype(BF16)
        r = r + ct[:, LANES:]
    r_ref[rows, :] = r
    acc_ref[rows, :] += _dot(jnp.concatenate(parts, axis=1), vblk)


def _sb_kernel(q_ref, k_ref, v_ref, uj_ref, o_ref, acc_ref, r_ref):
    i = pl.program_id(2)
    uj = uj_ref[...]
    tile = functools.partial(_sb_tile, q_ref, k_ref, v_ref, uj, r_ref, acc_ref)

    r_ref[...] = jnp.zeros_like(r_ref)
    acc_ref[...] = jnp.zeros_like(acc_ref)
    nt = TQ // TD
    d0 = pl.multiple_of(i * TQ, TQ)

    def key_start(kt):
        return pl.multiple_of(d0 + kt * TD, TD)

    def band(has_history):
        for rt in reversed(range(nt)):
            tile(rt * TD, TD, key_start(rt), TD, True)
            if rt > 0 or has_history:
                tile(rt * TD, TD, key_start(rt - 1), TD, False)

    @pl.when(i == 0)
    def _():
        band(False)

    @pl.when(i > 0)
    def _():
        band(True)

    def live():
        return jnp.max(r_ref[...]) > DEAD_STICK_LOG2

    @pl.when(jnp.logical_and(i > 0, live()))
    def _():
        for rt in range(1, nt):
            for kt in reversed(range(-1, rt - 1)):
                tile(rt * TD, TD, key_start(kt), TD, False)

        def cond(carry):
            kb, alive = carry
            return jnp.logical_and(kb >= 0, alive)

        def body(carry):
            kb, _ = carry
            tile(0, TQ, pl.multiple_of(kb * TD, TD), TD, False)
            return kb - 1, live()

        lax.while_loop(cond, body, (i * nt - 2, live()))

    o_ref[...] = acc_ref[...].astype(BF16)


def _suffix_sum_matrix():
    j = lax.broadcasted_iota(jnp.int32, (2 * LANES, 2 * LANES), 0) % LANES
    s = lax.broadcasted_iota(jnp.int32, (2 * LANES, 2 * LANES), 1)
    return jnp.where((s >= LANES) | (j > s), 1.0, 0.0).astype(BF16)


def _sb_attention(q, k, v, batch, seq):
    m, d = q.shape
    dh = d // N_HEADS
    q3, k3, v3 = (t.reshape(batch, seq, d) for t in (q, k, v))
    kv_spec = pl.BlockSpec((None, seq, dh), lambda b, h, i: (b, 0, h))
    qo_spec = pl.BlockSpec((None, TQ, dh), lambda b, h, i: (b, i, h))
    o = pl.pallas_call(
        _sb_kernel,
        grid=(batch, N_HEADS, seq // TQ),
        in_specs=[qo_spec, kv_spec, kv_spec,
                  pl.BlockSpec((2 * LANES, 2 * LANES), lambda b, h, i: (0, 0))],
        out_specs=qo_spec,
        out_shape=jax.ShapeDtypeStruct((batch, seq, d), BF16),
        scratch_shapes=[pltpu.VMEM((TQ, dh), F32), pltpu.VMEM((TQ, LANES), F32)],
        compiler_params=_params("arbitrary", "arbitrary", "arbitrary"),
        name="sb_attention",
    )(q3, k3, v3, _suffix_sum_matrix())
    return o.reshape(m, d)


def _proj_res_kernel(a_ref, w_ref, x_ref, o_ref):
    o_ref[...] = x_ref[...] + _dot(a_ref[...], w_ref[...])


def _proj_residual(a, w, layer, x):
    m, d = x.shape
    return pl.pallas_call(
        _proj_res_kernel,
        grid=(m // TM_RES,),
        in_specs=[
            pl.BlockSpec((TM_RES, d), lambda i: (i, 0)),
            pl.BlockSpec((None, d, d), lambda i: (layer, 0, 0)),
            pl.BlockSpec((TM_RES, d), lambda i: (i, 0)),
        ],
        out_specs=pl.BlockSpec((TM_RES, d), lambda i: (i, 0)),
        out_shape=jax.ShapeDtypeStruct((m, d), F32),
        compiler_params=_params("arbitrary"),
        name="proj_residual",
    )(a, w, x)


def _pool_kernel(x_ref, g_ref, w_ref, sc_ref, o_ref, buf_ref):
    j = pl.program_id(1)
    ts, d = x_ref.shape
    gw = d // len(POOL_WINDOWS)

    @pl.when(j == 0)
    def _():
        buf_ref[0:HALO, :] = jnp.zeros((HALO, d), F32)

    @pl.when(j > 0)
    def _():
        buf_ref[0:HALO, :] = buf_ref[ts:ts + HALO, :]

    x = x_ref[...]
    buf_ref[HALO:HALO + ts, :] = _rmsnorm(x, g_ref[...])
    pos1 = j * ts + 1 + lax.broadcasted_iota(jnp.int32, (ts, 1), 0)
    for gi, win in enumerate(POOL_WINDOWS):
        cols = slice(gi * gw, (gi + 1) * gw)
        cur = buf_ref[HALO:HALO + ts, cols]
        tot = cur
        for lag in range(1, win):
            tot = tot + buf_ref[HALO - lag:HALO - lag + ts, cols]
        cnt = jnp.minimum(pos1, win).astype(F32)
        piece = (tot / cnt - cur).astype(BF16)
        y = _dot(piece, w_ref[gi])
        o_ref[:, cols] = x[:, cols] + y * sc_ref[:, cols]


def _pool_mixer(x, g, w, layer, sc, batch, seq):
    m, d = x.shape
    ng = len(POOL_WINDOWS)
    gw = d // ng
    nt = seq // TS_POOL
    xspec = pl.BlockSpec((TS_POOL, d), lambda b, j: (b * nt + j, 0))
    vspec = pl.BlockSpec((1, d), lambda b, j: (0, 0))
    return pl.pallas_call(
        _pool_kernel,
        grid=(batch, nt),
        in_specs=[xspec, vspec,
                  pl.BlockSpec((None, ng, gw, gw), lambda b, j: (layer, 0, 0, 0)),
                  vspec],
        out_specs=xspec,
        out_shape=jax.ShapeDtypeStruct((m, d), F32),
        scratch_shapes=[pltpu.VMEM((HALO + TS_POOL, d), F32)],
        compiler_params=_params("arbitrary", "arbitrary"),
        name="pool_mixer",
    )(x, g, w, sc)


def _conv_in_kernel(tiles_per_seq, x_ref, g_ref, wb_ref, wc_ref, wu_ref, cw_ref,
                    o_ref, h_ref, carry_ref):
    i = pl.program_id(0)
    j = pl.program_id(1)

    @pl.when(j == 0)
    def _():
        h_ref[...] = _rmsnorm(x_ref[...], g_ref[...]).astype(BF16)

    @pl.when(i % tiles_per_seq == 0)
    def _():
        carry_ref[j] = jnp.zeros(carry_ref.shape[1:], F32)

    h = h_ref[...]
    gate = _dot(h, wb_ref[...])
    cu = _dot(h, wc_ref[...]) * _dot(h, wu_ref[...])
    tm = cu.shape[0]
    prev = carry_ref[j]
    p1 = prev[SUBLANES - 1:SUBLANES, :]
    p2 = prev[SUBLANES - 2:SUBLANES - 1, :]
    rows = lax.broadcasted_iota(jnp.int32, cu.shape, 0)
    lag1 = jnp.where(rows == 0, p1, pltpu.roll(cu, 1, 0))
    lag2 = jnp.where(rows == 0, p2, jnp.where(rows == 1, p1, pltpu.roll(cu, 2, 0)))
    cw = cw_ref[...]
    y = cw[0:1, :] * lag2 + cw[1:2, :] * lag1 + cw[2:3, :] * cu
    carry_ref[j] = cu[tm - SUBLANES:tm, :]
    o_ref[...] = (gate * y).astype(BF16)


def _conv_in(x, g, w3, layer, cw, seq):
    m, d = x.shape
    nj = d // TN_PROJ

    def wspec(part):
        return pl.BlockSpec((None, d, TN_PROJ),
                            lambda i, j: (layer, 0, part * nj + j))

    return pl.pallas_call(
        functools.partial(_conv_in_kernel, seq // TM_PROJ),
        grid=(m // TM_PROJ, nj),
        in_specs=[
            pl.BlockSpec((TM_PROJ, d), lambda i, j: (i, 0)),
            pl.BlockSpec((1, d), lambda i, j: (0, 0)),
            wspec(0), wspec(1), wspec(2),
            pl.BlockSpec((None, CONV_W, TN_PROJ), lambda i, j: (layer, 0, j)),
        ],
        out_specs=pl.BlockSpec((TM_PROJ, TN_PROJ), lambda i, j: (i, j)),
        out_shape=jax.ShapeDtypeStruct((m, d), BF16),
        scratch_shapes=[pltpu.VMEM((TM_PROJ, d), BF16),
                        pltpu.VMEM((nj, SUBLANES, TN_PROJ), F32)],
        compiler_params=_params("arbitrary", "arbitrary"),
        name="conv_in",
    )(x, g, w3, w3, w3, cw)


def _ffn_kernel(x_ref, g_ref, wg_ref, wu_ref, wd_ref, o_ref, h_ref):
    @pl.when(pl.program_id(1) == 0)
    def _():
        x = x_ref[...]
        h_ref[...] = _rmsnorm(x, g_ref[...]).astype(BF16)
        o_ref[...] = x

    h = h_ref[...]
    gate = _dot(h, wg_ref[...])
    up = _dot(h, wu_ref[...])
    act = (gate * jax.nn.sigmoid(gate) * up).astype(BF16)
    o_ref[...] += _dot(act, wd_ref[...])


def _ffn(x, g, wg, wu, wd, layer):
    m, d = x.shape
    f = wg.shape[2]
    return pl.pallas_call(
        _ffn_kernel,
        grid=(m // TM_FFN, f // TF_FFN),
        in_specs=[
            pl.BlockSpec((TM_FFN, d), lambda i, j: (i, 0)),
            pl.BlockSpec((1, d), lambda i, j: (0, 0)),
            pl.BlockSpec((None, d, TF_FFN), lambda i, j: (layer, 0, j)),
            pl.BlockSpec((None, d, TF_FFN), lambda i, j: (layer, 0, j)),
            pl.BlockSpec((None, TF_FFN, d), lambda i, j: (layer, j, 0)),
        ],
        out_specs=pl.BlockSpec((TM_FFN, d), lambda i, j: (i, 0)),
        out_shape=jax.ShapeDtypeStruct((m, d), F32),
        scratch_shapes=[pltpu.VMEM((TM_FFN, d), BF16)],
        compiler_params=_params("arbitrary", "arbitrary"),
        name="ffn",
    )(x, g, wg, wu, wd)


def kernel(x, norm_mix_g, norm_ffn_g, sb_w_qkv, sb_g_q, sb_g_k, sb_w_o, pool_w, pool_scale, conv_w_in, conv_w, conv_w_out, ffn_w_gate, ffn_w_up, ffn_w_down):
    batch, seq, d = x.shape
    depth = norm_mix_g.shape[0]
    xf = x.reshape(batch * seq, d)
    (sb_w_qkv, sb_w_o, pool_w, conv_w_in, conv_w_out, ffn_w_gate, ffn_w_up,
     ffn_w_down) = (w.astype(BF16) for w in (
         sb_w_qkv, sb_w_o, pool_w, conv_w_in, conv_w_out, ffn_w_gate, ffn_w_up,
         ffn_w_down))
    for i in range(depth):
        kind, j = i % N_MIXERS, i // N_MIXERS
        g_mix = norm_mix_g[i:i + 1]
        if kind == 0:
            q, k, v = _qkv_proj(xf, g_mix, sb_w_qkv, j, sb_g_q[j:j + 1], sb_g_k[j:j + 1])
            o = _sb_attention(q, k, v, batch, seq)
            xf = _proj_residual(o, sb_w_o, j, xf)
        elif kind == 1:
            xf = _pool_mixer(xf, g_mix, pool_w, j, pool_scale[j:j + 1], batch, seq)
        else:
            gated = _conv_in(xf, g_mix, conv_w_in, j, conv_w, seq)
            xf = _proj_residual(gated, conv_w_out, j, xf)
        xf = _ffn(xf, norm_ffn_g[i:i + 1], ffn_w_gate, ffn_w_up, ffn_w_down, i)
    return xf.reshape(batch, seq, d)
```

```python
import functools

import jax
import jax.numpy as jnp
from jax import lax
from jax.experimental import pallas as pl
from jax.experimental.pallas import tpu as pltpu

F32 = jnp.float32
BF16 = jnp.bfloat16

EPS = 1e-6
LOG2E = 1.4426950408889634
DEAD_STICK_LOG2 = -160.0
N_HEADS = 16
POOL_WINDOWS = (2, 4, 8, 16)
CONV_W = 3
N_MIXERS = 3

LANES = 128
SUBLANES = 8
VMEM_LIMIT_BYTES = 56 * 1024 * 1024

TM_PROJ = 1024
TN_PROJ = 512
TM_RES = 512
TM_FFN = 1024
TF_FFN = 512
TS_POOL = 512
TQ = 1024
TD = 256
HALO = 16


def _params(*sem):
    return pltpu.CompilerParams(dimension_semantics=sem,
                                vmem_limit_bytes=VMEM_LIMIT_BYTES)


def _rmsnorm(x, g):
    ms = jnp.mean(x * x, axis=-1, keepdims=True)
    return (x * lax.rsqrt(ms + EPS)) * g


def _dot(a, b):
    return jnp.dot(a, b, preferred_element_type=F32)


def _qkv_kernel(x_ref, g_ref, wq_ref, wk_ref, wv_ref, gq_ref, gk_ref,
                q_ref, k_ref, v_ref, h_ref):
    @pl.when(pl.program_id(1) == 0)
    def _():
        h_ref[...] = _rmsnorm(x_ref[...], g_ref[...]).astype(BF16)

    h = h_ref[...]

    def head_norm(y, gh, out_ref):
        for c in range(y.shape[1] // LANES):
            yc = y[:, c * LANES:(c + 1) * LANES]
            ms = jnp.mean(yc * yc, axis=-1, keepdims=True)
            out_ref[:, c * LANES:(c + 1) * LANES] = (
                yc * lax.rsqrt(ms + EPS) * gh).astype(BF16)

    head_norm(_dot(h, wq_ref[...]), gq_ref[...], q_ref)
    head_norm(_dot(h, wk_ref[...]), gk_ref[...], k_ref)
    v_ref[...] = _dot(h, wv_ref[...]).astype(BF16)


def _qkv_proj(x, g, w3, layer, gq, gk):
    m, d = x.shape
    nj = d // TN_PROJ
    out = jax.ShapeDtypeStruct((m, d), BF16)

    def wspec(part):
        return pl.BlockSpec((None, d, TN_PROJ),
                            lambda i, j: (layer, 0, part * nj + j))

    ospec = pl.BlockSpec((TM_PROJ, TN_PROJ), lambda i, j: (i, j))
    return pl.pallas_call(
        _qkv_kernel,
        grid=(m // TM_PROJ, nj),
        in_specs=[
            pl.BlockSpec((TM_PROJ, d), lambda i, j: (i, 0)),
            pl.BlockSpec((1, d), lambda i, j: (0, 0)),
            wspec(0), wspec(1), wspec(2),
            pl.BlockSpec((1, LANES), lambda i, j: (0, 0)),
            pl.BlockSpec((1, LANES), lambda i, j: (0, 0)),
        ],
        out_specs=[ospec, ospec, ospec],
        out_shape=[out, out, out],
        scratch_shapes=[pltpu.VMEM((TM_PROJ, d), BF16)],
        compiler_params=_params("arbitrary", "arbitrary"),
        name="qkv_proj",
    )(x, g, w3, w3, w3, gq, gk)


def _sb_logits(q_ref, k_ref, row0, nrows, k0, nkeys):
    q = q_ref[row0:row0 + nrows, :]
    kblk = k_ref[pl.ds(k0, nkeys), :]
    return lax.dot_general(q, kblk, (((1,), (1,)), ((), ())),
                           preferred_element_type=F32) * (q.shape[1] ** -0.5 * LOG2E)


def _sb_suffix_sums(w, uj, masked):
    groups = []
    for sb in range(w.shape[1] // LANES):
        ws = w[:, sb * LANES:(sb + 1) * LANES]
        lse = jnp.log2(1.0 + jnp.exp2(-jnp.abs(ws)))
        log_beta = jnp.minimum(ws, 0.0) - lse
        log_keep = log_beta - ws
        mask = None
        if masked:
            t_idx = lax.broadcasted_iota(jnp.int32, ws.shape, 0)
            s_idx = lax.broadcasted_iota(jnp.int32, ws.shape, 1) + sb * LANES
            mask = s_idx < t_idx
            log_keep = jnp.where(mask, log_keep, 0.0)
        hi = log_keep.astype(BF16)
        lo = (log_keep - hi.astype(F32)).astype(BF16)
        ct = _dot(jnp.concatenate([hi, lo], axis=1), uj)
        groups.append((log_beta, ct, mask))
    return groups


def _sb_weights(r, groups):
    parts = [None] * len(groups)
    for sb in reversed(range(len(groups))):
        log_beta, ct, mask = groups[sb]
        suffix, total = ct[:, :LANES], ct[:, LANES:]
        a = jnp.exp2(log_beta + (suffix if r is None else r + suffix))
        if mask is not None:
            a = jnp.where(mask, a, 0.0)
        parts[sb] = a.astype(BF16)
        r = total if r is None else r + total
    return r, jnp.concatenate(parts, axis=1)


def _sb_tile(q_ref, k_ref, v_ref, uj, r_ref, acc_ref, row0, nrows, k0, nkeys):
    rows = slice(row0, row0 + nrows)
    groups = _sb_suffix_sums(_sb_logits(q_ref, k_ref, row0, nrows, k0, nkeys), uj, False)
    r, a = _sb_weights(r_ref[rows, :], groups)
    r_ref[rows, :] = r
    acc_ref[rows, :] += _dot(a, v_ref[pl.ds(k0, nkeys), :])


def _sb_band(q_ref, k_ref, v_ref, uj, r_ref, acc_ref, d0, has_history):
    tiles = []
    for rt in range(TQ // TD):
        kts = [rt] + ([rt - 1] if rt > 0 or has_history else [])
        tiles.append((rt * TD, [pl.multiple_of(d0 + kt * TD, TD) for kt in kts]))
    logits = [[_sb_logits(q_ref, k_ref, row0, TD, k0, TD) for k0 in k0s]
              for row0, k0s in tiles]
    groups = [[_sb_suffix_sums(w, uj, n == 0) for n, w in enumerate(ws)] for ws in logits]
    for (row0, k0s), tile_groups in zip(tiles, groups):
        r, acc = None, None
        for k0, g in zip(k0s, tile_groups):
            r, a = _sb_weights(r, g)
            pv = _dot(a, v_ref[pl.ds(k0, TD), :])
            acc = pv if acc is None else acc + pv
        r_ref[row0:row0 + TD, :] = r
        acc_ref[row0:row0 + TD, :] = acc


def _sb_kernel(q_ref, k_ref, v_ref, uj_ref, o_ref, acc_ref, r_ref):
    i = pl.program_id(2)
    uj = uj_ref[...]
    refs = (q_ref, k_ref, v_ref, uj, r_ref, acc_ref)
    nt = TQ // TD
    d0 = pl.multiple_of(i * TQ, TQ)

    @pl.when(i == 0)
    def _():
        _sb_band(*refs, d0, False)

    @pl.when(i > 0)
    def _():
        _sb_band(*refs, d0, True)

    def live():
        return jnp.max(r_ref[...]) > DEAD_STICK_LOG2

    @pl.when(jnp.logical_and(i > 0, live()))
    def _():
        for rt in range(1, nt):
            for kt in reversed(range(-1, rt - 1)):
                _sb_tile(*refs, rt * TD, TD, pl.multiple_of(d0 + kt * TD, TD), TD)

        def cond(carry):
            kb, alive = carry
            return jnp.logical_and(kb >= 0, alive)

        def body(carry):
            kb, _ = carry
            _sb_tile(*refs, 0, TQ, pl.multiple_of(kb * TD, TD), TD)
            return kb - 1, live()

        lax.while_loop(cond, body, (i * nt - 2, live()))

    o_ref[...] = acc_ref[...].astype(BF16)


def _suffix_sum_matrix():
    j = lax.broadcasted_iota(jnp.int32, (2 * LANES, 2 * LANES), 0) % LANES
    s = lax.broadcasted_iota(jnp.int32, (2 * LANES, 2 * LANES), 1)
    return jnp.where((s >= LANES) | (j > s), 1.0, 0.0).astype(BF16)


def _sb_attention(q, k, v, batch, seq):
    m, d = q.shape
    dh = d // N_HEADS
    q3, k3, v3 = (t.reshape(batch, seq, d) for t in (q, k, v))
    kv_spec = pl.BlockSpec((None, seq, dh), lambda b, h, i: (b, 0, h))
    qo_spec = pl.BlockSpec((None, TQ, dh), lambda b, h, i: (b, i, h))
    o = pl.pallas_call(
        _sb_kernel,
        grid=(batch, N_HEADS, seq // TQ),
        in_specs=[qo_spec, kv_spec, kv_spec,
                  pl.BlockSpec((2 * LANES, 2 * LANES), lambda b, h, i: (0, 0))],
        out_specs=qo_spec,
        out_shape=jax.ShapeDtypeStruct((batch, seq, d), BF16),
        scratch_shapes=[pltpu.VMEM((TQ, dh), F32), pltpu.VMEM((TQ, LANES), F32)],
        compiler_params=_params("arbitrary", "arbitrary", "arbitrary"),
        name="sb_attention",
    )(q3, k3, v3, _suffix_sum_matrix())
    return o.reshape(m, d)


def _proj_res_kernel(a_ref, w_ref, x_ref, o_ref):
    o_ref[...] = x_ref[...] + _dot(a_ref[...], w_ref[...])


def _proj_residual(a, w, layer, x):
    m, d = x.shape
    return pl.pallas_call(
        _proj_res_kernel,
        grid=(m // TM_RES,),
        in_specs=[
            pl.BlockSpec((TM_RES, d), lambda i: (i, 0)),
            pl.BlockSpec((None, d, d), lambda i: (layer, 0, 0)),
            pl.BlockSpec((TM_RES, d), lambda i: (i, 0)),
        ],
        out_specs=pl.BlockSpec((TM_RES, d), lambda i: (i, 0)),
        out_shape=jax.ShapeDtypeStruct((m, d), F32),
        compiler_params=_params("arbitrary"),
        name="proj_residual",
    )(a, w, x)


def _pool_kernel(x_ref, g_ref, w_ref, sc_ref, o_ref, buf_ref):
    j = pl.program_id(1)
    ts, d = x_ref.shape
    gw = d // len(POOL_WINDOWS)

    @pl.when(j == 0)
    def _():
        buf_ref[0:HALO, :] = jnp.zeros((HALO, d), F32)

    @pl.when(j > 0)
    def _():
        buf_ref[0:HALO, :] = buf_ref[ts:ts + HALO, :]

    x = x_ref[...]
    buf_ref[HALO:HALO + ts, :] = _rmsnorm(x, g_ref[...])
    pos1 = j * ts + 1 + lax.broadcasted_iota(jnp.int32, (ts, 1), 0)
    for gi, win in enumerate(POOL_WINDOWS):
        cols = slice(gi * gw, (gi + 1) * gw)
        cur = buf_ref[HALO:HALO + ts, cols]
        tot = cur
        for lag in range(1, win):
            tot = tot + buf_ref[HALO - lag:HALO - lag + ts, cols]
        cnt = jnp.minimum(pos1, win).astype(F32)
        piece = (tot / cnt - cur).astype(BF16)
        y = _dot(piece, w_ref[gi])
        o_ref[:, cols] = x[:, cols] + y * sc_ref[:, cols]


def _pool_mixer(x, g, w, layer, sc, batch, seq):
    m, d = x.shape
    ng = len(POOL_WINDOWS)
    gw = d // ng
    nt = seq // TS_POOL
    xspec = pl.BlockSpec((TS_POOL, d), lambda b, j: (b * nt + j, 0))
    vspec = pl.BlockSpec((1, d), lambda b, j: (0, 0))
    return pl.pallas_call(
        _pool_kernel,
        grid=(batch, nt),
        in_specs=[xspec, vspec,
                  pl.BlockSpec((None, ng, gw, gw), lambda b, j: (layer, 0, 0, 0)),
                  vspec],
        out_specs=xspec,
        out_shape=jax.ShapeDtypeStruct((m, d), F32),
        scratch_shapes=[pltpu.VMEM((HALO + TS_POOL, d), F32)],
        compiler_params=_params("arbitrary", "arbitrary"),
        name="pool_mixer",
    )(x, g, w, sc)


def _conv_in_kernel(tiles_per_seq, x_ref, g_ref, wb_ref, wc_ref, wu_ref, cw_ref,
                    o_ref, h_ref, carry_ref):
    i = pl.program_id(0)
    j = pl.program_id(1)

    @pl.when(j == 0)
    def _():
        h_ref[...] = _rmsnorm(x_ref[...], g_ref[...]).astype(BF16)

    @pl.when(i % tiles_per_seq == 0)
    def _():
        carry_ref[j] = jnp.zeros(carry_ref.shape[1:], F32)

    h = h_ref[...]
    gate = _dot(h, wb_ref[...])
    cu = _dot(h, wc_ref[...]) * _dot(h, wu_ref[...])
    tm = cu.shape[0]
    prev = carry_ref[j]
    p1 = prev[SUBLANES - 1:SUBLANES, :]
    p2 = prev[SUBLANES - 2:SUBLANES - 1, :]
    rows = lax.broadcasted_iota(jnp.int32, cu.shape, 0)
    lag1 = jnp.where(rows == 0, p1, pltpu.roll(cu, 1, 0))
    lag2 = jnp.where(rows == 0, p2, jnp.where(rows == 1, p1, pltpu.roll(cu, 2, 0)))
    cw = cw_ref[...]
    y = cw[0:1, :] * lag2 + cw[1:2, :] * lag1 + cw[2:3, :] * cu
    carry_ref[j] = cu[tm - SUBLANES:tm, :]
    o_ref[...] = (gate * y).astype(BF16)


def _conv_in(x, g, w3, layer, cw, seq):
    m, d = x.shape
    nj = d // TN_PROJ

    def wspec(part):
        return pl.BlockSpec((None, d, TN_PROJ),
                            lambda i, j: (layer, 0, part * nj + j))

    return pl.pallas_call(
        functools.partial(_conv_in_kernel, seq // TM_PROJ),
        grid=(m // TM_PROJ, nj),
        in_specs=[
            pl.BlockSpec((TM_PROJ, d), lambda i, j: (i, 0)),
            pl.BlockSpec((1, d), lambda i, j: (0, 0)),
            wspec(0), wspec(1), wspec(2),
            pl.BlockSpec((None, CONV_W, TN_PROJ), lambda i, j: (layer, 0, j)),
        ],
        out_specs=pl.BlockSpec((TM_PROJ, TN_PROJ), lambda i, j: (i, j)),
        out_shape=jax.ShapeDtypeStruct((m, d), BF16),
        scratch_shapes=[pltpu.VMEM((TM_PROJ, d), BF16),
                        pltpu.VMEM((nj, SUBLANES, TN_PROJ), F32)],
        compiler_params=_params("arbitrary", "arbitrary"),
        name="conv_in",
    )(x, g, w3, w3, w3, cw)


def _ffn_kernel(x_ref, g_ref, wg_ref, wu_ref, wd_ref, o_ref, h_ref):
    @pl.when(pl.program_id(1) == 0)
    def _():
        x = x_ref[...]
        h_ref[...] = _rmsnorm(x, g_ref[...]).astype(BF16)
        o_ref[...] = x

    h = h_ref[...]
    gate = _dot(h, wg_ref[...])
    up = _dot(h, wu_ref[...])
    act = (gate * jax.nn.sigmoid(gate) * up).astype(BF16)
    o_ref[...] += _dot(act, wd_ref[...])


def _ffn(x, g, wg, wu, wd, layer):
    m, d = x.shape
    f = wg.shape[2]
    return pl.pallas_call(
        _ffn_kernel,
        grid=(m // TM_FFN, f // TF_FFN),
        in_specs=[
            pl.BlockSpec((TM_FFN, d), lambda i, j: (i, 0)),
            pl.BlockSpec((1, d), lambda i, j: (0, 0)),
            pl.BlockSpec((None, d, TF_FFN), lambda i, j: (layer, 0, j)),
            pl.BlockSpec((None, d, TF_FFN), lambda i, j: (layer, 0, j)),
            pl.BlockSpec((None, TF_FFN, d), lambda i, j: (layer, j, 0)),
        ],
        out_specs=pl.BlockSpec((TM_FFN, d), lambda i, j: (i, 0)),
        out_shape=jax.ShapeDtypeStruct((m, d), F32),
        scratch_shapes=[pltpu.VMEM((TM_FFN, d), BF16)],
        compiler_params=_params("arbitrary", "arbitrary"),
        name="ffn",
    )(x, g, wg, wu, wd)


def kernel(x, norm_mix_g, norm_ffn_g, sb_w_qkv, sb_g_q, sb_g_k, sb_w_o, pool_w, pool_scale, conv_w_in, conv_w, conv_w_out, ffn_w_gate, ffn_w_up, ffn_w_down):
    batch, seq, d = x.shape
    depth = norm_mix_g.shape[0]
    xf = x.reshape(batch * seq, d)
    (sb_w_qkv, sb_w_o, pool_w, conv_w_in, conv_w_out, ffn_w_gate, ffn_w_up,
     ffn_w_down) = (w.astype(BF16) for w in (
         sb_w_qkv, sb_w_o, pool_w, conv_w_in, conv_w_out, ffn_w_gate, ffn_w_up,
         ffn_w_down))
    for i in range(depth):
        kind, j = i % N_MIXERS, i // N_MIXERS
        g_mix = norm_mix_g[i:i + 1]
        if kind == 0:
            q, k, v = _qkv_proj(xf, g_mix, sb_w_qkv, j, sb_g_q[j:j + 1], sb_g_k[j:j + 1])
            o = _sb_attention(q, k, v, batch, seq)
            xf = _proj_residual(o, sb_w_o, j, xf)
        elif kind == 1:
            xf = _pool_mixer(xf, g_mix, pool_w, j, pool_scale[j:j + 1], batch, seq)
        else:
            gated = _conv_in(xf, g_mix, conv_w_in, j, conv_w, seq)
            xf = _proj_residual(gated, conv_w_out, j, xf)
        xf = _ffn(xf, norm_ffn_g[i:i + 1], ffn_w_gate, ffn_w_up, ffn_w_down, i)
    return xf.reshape(batch, seq, d)
```

```python
import functools

import jax
import jax.numpy as jnp
from jax import lax
from jax.experimental import pallas as pl
from jax.experimental.pallas import tpu as pltpu

F32 = jnp.float32
BF16 = jnp.bfloat16

EPS = 1e-6
LOG2E = 1.4426950408889634
DEAD_STICK_LOG2 = -160.0
N_HEADS = 16
POOL_WINDOWS = (2, 4, 8, 16)
CONV_W = 3
N_MIXERS = 3

LANES = 128
SUBLANES = 8
VMEM_LIMIT_BYTES = 56 * 1024 * 1024

TM_PROJ = 1024
TN_PROJ = 512
TM_RES = 512
TM_FFN = 1024
TF_FFN = 512
TF_HEAD = 256
TS_POOL = 512
TQ = 1024
TD = 256
HALO = 16


def _params(*sem):
    return pltpu.CompilerParams(dimension_semantics=sem,
                                vmem_limit_bytes=VMEM_LIMIT_BYTES)


def _rmsnorm(x, g):
    ms = jnp.mean(x * x, axis=-1, keepdims=True)
    return (x * lax.rsqrt(ms + EPS)) * g


def _dot(a, b):
    return jnp.dot(a, b, preferred_element_type=F32)


def _qkv_kernel(x_ref, g_ref, wq_ref, wk_ref, wv_ref, gq_ref, gk_ref,
                q_ref, k_ref, v_ref, h_ref):
    def head_norm(y, gh, out_ref):
        for c in range(y.shape[1] // LANES):
            yc = y[:, c * LANES:(c + 1) * LANES]
            ms = jnp.mean(yc * yc, axis=-1, keepdims=True)
            out_ref[:, c * LANES:(c + 1) * LANES] = (
                yc * lax.rsqrt(ms + EPS) * gh).astype(BF16)

    def project(h):
        head_norm(_dot(h, wq_ref[...]), gq_ref[...], q_ref)
        head_norm(_dot(h, wk_ref[...]), gk_ref[...], k_ref)
        v_ref[...] = _dot(h, wv_ref[...]).astype(BF16)

    @pl.when(pl.program_id(1) == 0)
    def _():
        h = _rmsnorm(x_ref[...], g_ref[...]).astype(BF16)
        h_ref[...] = h
        project(h)

    @pl.when(pl.program_id(1) > 0)
    def _():
        project(h_ref[...])


def _qkv_proj(x, g, w3, layer, gq, gk):
    m, d = x.shape
    nj = d // TN_PROJ
    out = jax.ShapeDtypeStruct((m, d), BF16)

    def wspec(part):
        return pl.BlockSpec((None, d, TN_PROJ),
                            lambda i, j: (layer, 0, part * nj + j))

    ospec = pl.BlockSpec((TM_PROJ, TN_PROJ), lambda i, j: (i, j))
    return pl.pallas_call(
        _qkv_kernel,
        grid=(m // TM_PROJ, nj),
        in_specs=[
            pl.BlockSpec((TM_PROJ, d), lambda i, j: (i, 0)),
            pl.BlockSpec((1, d), lambda i, j: (0, 0)),
            wspec(0), wspec(1), wspec(2),
            pl.BlockSpec((1, LANES), lambda i, j: (0, 0)),
            pl.BlockSpec((1, LANES), lambda i, j: (0, 0)),
        ],
        out_specs=[ospec, ospec, ospec],
        out_shape=[out, out, out],
        scratch_shapes=[pltpu.VMEM((TM_PROJ, d), BF16)],
        compiler_params=_params("arbitrary", "arbitrary"),
        name="qkv_proj",
    )(x, g, w3, w3, w3, gq, gk)


def _sb_logits(q_ref, k_ref, row0, nrows, k0, nkeys):
    q = q_ref[row0:row0 + nrows, :]
    kblk = k_ref[pl.ds(k0, nkeys), :]
    return lax.dot_general(q, kblk, (((1,), (1,)), ((), ())),
                           preferred_element_type=F32) * (q.shape[1] ** -0.5 * LOG2E)


def _sb_suffix_sums(w, uj, masked):
    groups = []
    for sb in range(w.shape[1] // LANES):
        ws = w[:, sb * LANES:(sb + 1) * LANES]
        lse = jnp.log2(1.0 + jnp.exp2(-jnp.abs(ws)))
        log_beta = jnp.minimum(ws, 0.0) - lse
        log_keep = log_beta - ws
        mask = None
        if masked:
            t_idx = lax.broadcasted_iota(jnp.int32, ws.shape, 0)
            s_idx = lax.broadcasted_iota(jnp.int32, ws.shape, 1) + sb * LANES
            mask = s_idx < t_idx
            log_keep = jnp.where(mask, log_keep, 0.0)
        hi = log_keep.astype(BF16)
        lo = (log_keep - hi.astype(F32)).astype(BF16)
        ct = _dot(jnp.concatenate([hi, lo], axis=1), uj)
        groups.append((log_beta, ct, mask))
    return groups


def _sb_weights(r, groups):
    parts = [None] * len(groups)
    for sb in reversed(range(len(groups))):
        log_beta, ct, mask = groups[sb]
        suffix, total = ct[:, :LANES], ct[:, LANES:]
        a = jnp.exp2(log_beta + (suffix if r is None else r + suffix))
        if mask is not None:
            a = jnp.where(mask, a, 0.0)
        parts[sb] = a.astype(BF16)
        r = total if r is None else r + total
    return r, jnp.concatenate(parts, axis=1)


def _sb_tile(q_ref, k_ref, v_ref, uj, r_ref, acc_ref, row0, nrows, k0, nkeys):
    rows = slice(row0, row0 + nrows)
    groups = _sb_suffix_sums(_sb_logits(q_ref, k_ref, row0, nrows, k0, nkeys), uj, False)
    r, a = _sb_weights(r_ref[rows, :], groups)
    r_ref[rows, :] = r
    acc_ref[rows, :] += _dot(a, v_ref[pl.ds(k0, nkeys), :])


def _sb_band(q_ref, k_ref, v_ref, uj, r_ref, acc_ref, d0, has_history):
    tiles = []
    for rt in range(TQ // TD):
        kts = [rt] + ([rt - 1] if rt > 0 or has_history else [])
        tiles.append((rt * TD, [pl.multiple_of(d0 + kt * TD, TD) for kt in kts]))
    logits = [[_sb_logits(q_ref, k_ref, row0, TD, k0, TD) for k0 in k0s]
              for row0, k0s in tiles]
    groups = [[_sb_suffix_sums(w, uj, n == 0) for n, w in enumerate(ws)] for ws in logits]
    for (row0, k0s), tile_groups in zip(tiles, groups):
        r, acc = None, None
        for k0, g in zip(k0s, tile_groups):
            r, a = _sb_weights(r, g)
            pv = _dot(a, v_ref[pl.ds(k0, TD), :])
            acc = pv if acc is None else acc + pv
        r_ref[row0:row0 + TD, :] = r
        acc_ref[row0:row0 + TD, :] = acc


def _sb_kernel(q_ref, k_ref, v_ref, uj_ref, o_ref, acc_ref, r_ref):
    i = pl.program_id(2)
    uj = uj_ref[...]
    refs = (q_ref, k_ref, v_ref, uj, r_ref, acc_ref)
    nt = TQ // TD
    d0 = pl.multiple_of(i * TQ, TQ)

    @pl.when(i == 0)
    def _():
        _sb_band(*refs, d0, False)

    @pl.when(i > 0)
    def _():
        _sb_band(*refs, d0, True)

    def live():
        return jnp.max(r_ref[...]) > DEAD_STICK_LOG2

    @pl.when(jnp.logical_and(i > 0, live()))
    def _():
        for rt in range(1, nt):
            for kt in reversed(range(-1, rt - 1)):
                _sb_tile(*refs, rt * TD, TD, pl.multiple_of(d0 + kt * TD, TD), TD)

        def cond(carry):
            kb, alive = carry
            return jnp.logical_and(kb >= 0, alive)

        def body(carry):
            kb, _ = carry
            _sb_tile(*refs, 0, TQ, pl.multiple_of(kb * TD, TD), TD)
            return kb - 1, live()

        lax.while_loop(cond, body, (i * nt - 2, live()))

    o_ref[...] = acc_ref[...].astype(BF16)


def _suffix_sum_matrix():
    j = lax.broadcasted_iota(jnp.int32, (2 * LANES, 2 * LANES), 0) % LANES
    s = lax.broadcasted_iota(jnp.int32, (2 * LANES, 2 * LANES), 1)
    return jnp.where((s >= LANES) | (j > s), 1.0, 0.0).astype(BF16)


def _sb_attention(q, k, v, batch, seq):
    m, d = q.shape
    dh = d // N_HEADS
    q3, k3, v3 = (t.reshape(batch, seq, d) for t in (q, k, v))
    kv_spec = pl.BlockSpec((None, seq, dh), lambda b, h, i: (b, 0, h))
    qo_spec = pl.BlockSpec((None, TQ, dh), lambda b, h, i: (b, i, h))
    o = pl.pallas_call(
        _sb_kernel,
        grid=(batch, N_HEADS, seq // TQ),
        in_specs=[qo_spec, kv_spec, kv_spec,
                  pl.BlockSpec((2 * LANES, 2 * LANES), lambda b, h, i: (0, 0))],
        out_specs=qo_spec,
        out_shape=jax.ShapeDtypeStruct((batch, seq, d), BF16),
        scratch_shapes=[pltpu.VMEM((TQ, dh), F32), pltpu.VMEM((TQ, LANES), F32)],
        compiler_params=_params("arbitrary", "arbitrary", "arbitrary"),
        name="sb_attention",
    )(q3, k3, v3, _suffix_sum_matrix())
    return o.reshape(m, d)


def _proj_res_kernel(a_ref, w_ref, x_ref, o_ref):
    o_ref[...] = x_ref[...] + _dot(a_ref[...], w_ref[...])


def _proj_residual(a, w, layer, x):
    m, d = x.shape
    return pl.pallas_call(
        _proj_res_kernel,
        grid=(m // TM_RES,),
        in_specs=[
            pl.BlockSpec((TM_RES, d), lambda i: (i, 0)),
            pl.BlockSpec((None, d, d), lambda i: (layer, 0, 0)),
            pl.BlockSpec((TM_RES, d), lambda i: (i, 0)),
        ],
        out_specs=pl.BlockSpec((TM_RES, d), lambda i: (i, 0)),
        out_shape=jax.ShapeDtypeStruct((m, d), F32),
        compiler_params=_params("arbitrary"),
        name="proj_residual",
    )(a, w, x)


def _pool_kernel(x_ref, g_ref, w_ref, sc_ref, o_ref, buf_ref):
    j = pl.program_id(1)
    ts, d = x_ref.shape
    gw = d // len(POOL_WINDOWS)

    @pl.when(j == 0)
    def _():
        buf_ref[0:HALO, :] = jnp.zeros((HALO, d), F32)

    @pl.when(j > 0)
    def _():
        buf_ref[0:HALO, :] = buf_ref[ts:ts + HALO, :]

    x = x_ref[...]
    buf_ref[HALO:HALO + ts, :] = _rmsnorm(x, g_ref[...])
    pos1 = j * ts + 1 + lax.broadcasted_iota(jnp.int32, (ts, 1), 0)
    for gi, win in enumerate(POOL_WINDOWS):
        cols = slice(gi * gw, (gi + 1) * gw)
        cur = buf_ref[HALO:HALO + ts, cols]
        tot = cur
        for lag in range(1, win):
            tot = tot + buf_ref[HALO - lag:HALO - lag + ts, cols]
        cnt = jnp.minimum(pos1, win).astype(F32)
        piece = (tot / cnt - cur).astype(BF16)
        y = _dot(piece, w_ref[gi])
        o_ref[:, cols] = x[:, cols] + y * sc_ref[:, cols]


def _pool_mixer(x, g, w, layer, sc, batch, seq):
    m, d = x.shape
    ng = len(POOL_WINDOWS)
    gw = d // ng
    nt = seq // TS_POOL
    xspec = pl.BlockSpec((TS_POOL, d), lambda b, j: (b * nt + j, 0))
    vspec = pl.BlockSpec((1, d), lambda b, j: (0, 0))
    return pl.pallas_call(
        _pool_kernel,
        grid=(batch, nt),
        in_specs=[xspec, vspec,
                  pl.BlockSpec((None, ng, gw, gw), lambda b, j: (layer, 0, 0, 0)),
                  vspec],
        out_specs=xspec,
        out_shape=jax.ShapeDtypeStruct((m, d), F32),
        scratch_shapes=[pltpu.VMEM((HALO + TS_POOL, d), F32)],
        compiler_params=_params("arbitrary", "arbitrary"),
        name="pool_mixer",
    )(x, g, w, sc)


def _conv_in_kernel(tiles_per_seq, x_ref, g_ref, wb_ref, wc_ref, wu_ref, cw_ref,
                    o_ref, h_ref, carry_ref):
    i = pl.program_id(0)
    j = pl.program_id(1)

    @pl.when(i % tiles_per_seq == 0)
    def _():
        carry_ref[j] = jnp.zeros(carry_ref.shape[1:], F32)

    def gated_conv(h):
        gate = _dot(h, wb_ref[...])
        cu = _dot(h, wc_ref[...]) * _dot(h, wu_ref[...])
        tm = cu.shape[0]
        prev = carry_ref[j]
        p1 = prev[SUBLANES - 1:SUBLANES, :]
        p2 = prev[SUBLANES - 2:SUBLANES - 1, :]
        rows = lax.broadcasted_iota(jnp.int32, cu.shape, 0)
        lag1 = jnp.where(rows == 0, p1, pltpu.roll(cu, 1, 0))
        lag2 = jnp.where(rows == 0, p2, jnp.where(rows == 1, p1, pltpu.roll(cu, 2, 0)))
        cw = cw_ref[...]
        y = cw[0:1, :] * lag2 + cw[1:2, :] * lag1 + cw[2:3, :] * cu
        carry_ref[j] = cu[tm - SUBLANES:tm, :]
        o_ref[...] = (gate * y).astype(BF16)

    @pl.when(j == 0)
    def _():
        h = _rmsnorm(x_ref[...], g_ref[...]).astype(BF16)
        h_ref[...] = h
        gated_conv(h)

    @pl.when(j > 0)
    def _():
        gated_conv(h_ref[...])


def _conv_in(x, g, w3, layer, cw, seq):
    m, d = x.shape
    nj = d // TN_PROJ

    def wspec(part):
        return pl.BlockSpec((None, d, TN_PROJ),
                            lambda i, j: (layer, 0, part * nj + j))

    return pl.pallas_call(
        functools.partial(_conv_in_kernel, seq // TM_PROJ),
        grid=(m // TM_PROJ, nj),
        in_specs=[
            pl.BlockSpec((TM_PROJ, d), lambda i, j: (i, 0)),
            pl.BlockSpec((1, d), lambda i, j: (0, 0)),
            wspec(0), wspec(1), wspec(2),
            pl.BlockSpec((None, CONV_W, TN_PROJ), lambda i, j: (layer, 0, j)),
        ],
        out_specs=pl.BlockSpec((TM_PROJ, TN_PROJ), lambda i, j: (i, j)),
        out_shape=jax.ShapeDtypeStruct((m, d), BF16),
        scratch_shapes=[pltpu.VMEM((TM_PROJ, d), BF16),
                        pltpu.VMEM((nj, SUBLANES, TN_PROJ), F32)],
        compiler_params=_params("arbitrary", "arbitrary"),
        name="conv_in",
    )(x, g, w3, w3, w3, cw)


def _ffn_step(x_ref, g_ref, weights, o_ref, h_ref):
    def hidden_tile(h):
        wg, wu, wd = weights()
        gate = _dot(h, wg)
        up = _dot(h, wu)
        act = (gate * jax.nn.sigmoid(gate) * up).astype(BF16)
        return _dot(act, wd)

    @pl.when(pl.program_id(1) == 0)
    def _():
        x = x_ref[...]
        h = _rmsnorm(x, g_ref[...]).astype(BF16)
        h_ref[...] = h
        o_ref[...] = x + hidden_tile(h)

    @pl.when(pl.program_id(1) > 0)
    def _():
        o_ref[...] += hidden_tile(h_ref[...])


def _ffn_head_kernel(x_ref, g_ref, wg_ref, wu_ref, wd_ref,
                     o_ref, wg_out, wu_out, wd_out, h_ref):
    def weights():
        wg, wu, wd = (w[...].astype(BF16) for w in (wg_ref, wu_ref, wd_ref))
        wg_out[...], wu_out[...], wd_out[...] = wg, wu, wd
        return wg, wu, wd

    _ffn_step(x_ref, g_ref, weights, o_ref, h_ref)


def _ffn_kernel(x_ref, g_ref, wg_ref, wu_ref, wd_ref, head_ref, o_ref, h_ref):
    del head_ref
    _ffn_step(x_ref, g_ref, lambda: (wg_ref[...], wu_ref[...], wd_ref[...]), o_ref, h_ref)


def _ffn(x, g, wg32, wu32, wd32, layer):
    m, d = x.shape
    f = wg32.shape[2]
    once = pl.Buffered(1)
    y, wg, wu, wd = pl.pallas_call(
        _ffn_head_kernel,
        grid=(1, f // TF_HEAD),
        in_specs=[
            pl.BlockSpec((TM_FFN, d), lambda i, j: (0, 0), pipeline_mode=once),
            pl.BlockSpec((1, d), lambda i, j: (0, 0)),
            pl.BlockSpec((None, d, TF_HEAD), lambda i, j: (layer, 0, j)),
            pl.BlockSpec((None, d, TF_HEAD), lambda i, j: (layer, 0, j)),
            pl.BlockSpec((None, TF_HEAD, d), lambda i, j: (layer, j, 0)),
        ],
        out_specs=[
            pl.BlockSpec((TM_FFN, d), lambda i, j: (0, 0)),
            pl.BlockSpec((d, TF_HEAD), lambda i, j: (0, j)),
            pl.BlockSpec((d, TF_HEAD), lambda i, j: (0, j)),
            pl.BlockSpec((TF_HEAD, d), lambda i, j: (j, 0)),
        ],
        out_shape=[jax.ShapeDtypeStruct((m, d), F32),
                   jax.ShapeDtypeStruct((d, f), BF16),
                   jax.ShapeDtypeStruct((d, f), BF16),
                   jax.ShapeDtypeStruct((f, d), BF16)],
        scratch_shapes=[pltpu.VMEM((TM_FFN, d), BF16)],
        compiler_params=_params("arbitrary", "arbitrary"),
        name="ffn_head",
    )(x, g, wg32, wu32, wd32)
    return pl.pallas_call(
        _ffn_kernel,
        grid=(m // TM_FFN - 1, f // TF_FFN),
        in_specs=[
            pl.BlockSpec((TM_FFN, d), lambda i, j: (i + 1, 0)),
            pl.BlockSpec((1, d), lambda i, j: (0, 0)),
            pl.BlockSpec((d, TF_FFN), lambda i, j: (0, j)),
            pl.BlockSpec((d, TF_FFN), lambda i, j: (0, j)),
            pl.BlockSpec((TF_FFN, d), lambda i, j: (j, 0)),
            pl.BlockSpec(memory_space=pl.ANY),
        ],
        out_specs=pl.BlockSpec((TM_FFN, d), lambda i, j: (i + 1, 0)),
        out_shape=jax.ShapeDtypeStruct((m, d), F32),
        input_output_aliases={5: 0},
        scratch_shapes=[pltpu.VMEM((TM_FFN, d), BF16)],
        compiler_params=_params("arbitrary", "arbitrary"),
        name="ffn",
    )(x, g, wg, wu, wd, y)


def kernel(x, norm_mix_g, norm_ffn_g, sb_w_qkv, sb_g_q, sb_g_k, sb_w_o, pool_w, pool_scale, conv_w_in, conv_w, conv_w_out, ffn_w_gate, ffn_w_up, ffn_w_down):
    batch, seq, d = x.shape
    depth = norm_mix_g.shape[0]
    xf = x.reshape(batch * seq, d)
    sb_w_qkv, sb_w_o, pool_w, conv_w_in, conv_w_out = (
        w.astype(BF16) for w in (sb_w_qkv, sb_w_o, pool_w, conv_w_in, conv_w_out))
    for i in range(depth):
        kind, j = i % N_MIXERS, i // N_MIXERS
        g_mix = norm_mix_g[i:i + 1]
        if kind == 0:
            q, k, v = _qkv_proj(xf, g_mix, sb_w_qkv, j, sb_g_q[j:j + 1], sb_g_k[j:j + 1])
            o = _sb_attention(q, k, v, batch, seq)
            xf = _proj_residual(o, sb_w_o, j, xf)
        elif kind == 1:
            xf = _pool_mixer(xf, g_mix, pool_w, j, pool_scale[j:j + 1], batch, seq)
        else:
            gated = _conv_in(xf, g_mix, conv_w_in, j, conv_w, seq)
            xf = _proj_residual(gated, conv_w_out, j, xf)
        xf = _ffn(xf, norm_ffn_g[i:i + 1], ffn_w_gate, ffn_w_up, ffn_w_down, i)
    return xf.reshape(batch, seq, d)
```

```python
import functools

import jax
import jax.numpy as jnp
from jax import lax
from jax.experimental import pallas as pl
from jax.experimental.pallas import tpu as pltpu

F32 = jnp.float32
BF16 = jnp.bfloat16

EPS = 1e-6
LOG2E = 1.4426950408889634
DEAD_STICK_LOG2 = -160.0
N_HEADS = 16
POOL_WINDOWS = (2, 4, 8, 16)
CONV_W = 3
N_MIXERS = 3

LANES = 128
SUBLANES = 8
VMEM_LIMIT_BYTES = 56 * 1024 * 1024

TM_PROJ = 1024
TN_PROJ = 512
TM_RES = 512
TM_FFN = 1024
TF_FFN = 512
TF_HEAD = 256
TS_POOL = 512
TQ = 1024
TD = 256
HEADS_PER_STEP = 2
HALO = 16


def _params(*sem):
    return pltpu.CompilerParams(dimension_semantics=sem,
                                vmem_limit_bytes=VMEM_LIMIT_BYTES)


def _rmsnorm(x, g):
    ms = jnp.mean(x * x, axis=-1, keepdims=True)
    return (x * lax.rsqrt(ms + EPS)) * g


def _dot(a, b):
    return jnp.dot(a, b, preferred_element_type=F32)


def _qkv_kernel(x_ref, g_ref, wq_ref, wk_ref, wv_ref, gq_ref, gk_ref,
                q_ref, k_ref, v_ref, h_ref):
    def head_norm(y, gh, out_ref):
        for c in range(y.shape[1] // LANES):
            yc = y[:, c * LANES:(c + 1) * LANES]
            ms = jnp.mean(yc * yc, axis=-1, keepdims=True)
            out_ref[:, c * LANES:(c + 1) * LANES] = (
                yc * lax.rsqrt(ms + EPS) * gh).astype(BF16)

    def project(h):
        head_norm(_dot(h, wq_ref[...]), gq_ref[...], q_ref)
        head_norm(_dot(h, wk_ref[...]), gk_ref[...], k_ref)
        v_ref[...] = _dot(h, wv_ref[...]).astype(BF16)

    @pl.when(pl.program_id(1) == 0)
    def _():
        h = _rmsnorm(x_ref[...], g_ref[...]).astype(BF16)
        h_ref[...] = h
        project(h)

    @pl.when(pl.program_id(1) > 0)
    def _():
        project(h_ref[...])


def _qkv_proj(x, g, w3, layer, gq, gk):
    m, d = x.shape
    nj = d // TN_PROJ
    out = jax.ShapeDtypeStruct((m, d), BF16)

    def wspec(part):
        return pl.BlockSpec((None, d, TN_PROJ),
                            lambda i, j: (layer, 0, part * nj + j))

    ospec = pl.BlockSpec((TM_PROJ, TN_PROJ), lambda i, j: (i, j))
    return pl.pallas_call(
        _qkv_kernel,
        grid=(m // TM_PROJ, nj),
        in_specs=[
            pl.BlockSpec((TM_PROJ, d), lambda i, j: (i, 0)),
            pl.BlockSpec((1, d), lambda i, j: (0, 0)),
            wspec(0), wspec(1), wspec(2),
            pl.BlockSpec((1, LANES), lambda i, j: (0, 0)),
            pl.BlockSpec((1, LANES), lambda i, j: (0, 0)),
        ],
        out_specs=[ospec, ospec, ospec],
        out_shape=[out, out, out],
        scratch_shapes=[pltpu.VMEM((TM_PROJ, d), BF16)],
        compiler_params=_params("arbitrary", "arbitrary"),
        name="qkv_proj",
    )(x, g, w3, w3, w3, gq, gk)


def _sb_logits(q_ref, k_ref, row0, nrows, k0, nkeys):
    q = q_ref[row0:row0 + nrows, :]
    kblk = k_ref[pl.ds(k0, nkeys), :]
    return lax.dot_general(q, kblk, (((1,), (1,)), ((), ())),
                           preferred_element_type=F32) * (q.shape[1] ** -0.5 * LOG2E)


def _sb_suffix_sums(w, uj, masked):
    groups = []
    for sb in range(w.shape[1] // LANES):
        ws = w[:, sb * LANES:(sb + 1) * LANES]
        lse = jnp.log2(1.0 + jnp.exp2(-jnp.abs(ws)))
        log_beta = jnp.minimum(ws, 0.0) - lse
        log_keep = log_beta - ws
        mask = None
        if masked:
            t_idx = lax.broadcasted_iota(jnp.int32, ws.shape, 0)
            s_idx = lax.broadcasted_iota(jnp.int32, ws.shape, 1) + sb * LANES
            mask = s_idx < t_idx
            log_keep = jnp.where(mask, log_keep, 0.0)
        hi = log_keep.astype(BF16)
        lo = (log_keep - hi.astype(F32)).astype(BF16)
        ct = _dot(jnp.concatenate([hi, lo], axis=1), uj)
        groups.append((log_beta, ct, mask))
    return groups


def _sb_weights(r, groups):
    parts = [None] * len(groups)
    for sb in reversed(range(len(groups))):
        log_beta, ct, mask = groups[sb]
        suffix, total = ct[:, :LANES], ct[:, LANES:]
        a = jnp.exp2(log_beta + (suffix if r is None else r + suffix))
        if mask is not None:
            a = jnp.where(mask, a, 0.0)
        parts[sb] = a.astype(BF16)
        r = total if r is None else r + total
    return r, jnp.concatenate(parts, axis=1)


def _sb_tile(head, uj, row0, nrows, k0, nkeys):
    q_ref, k_ref, v_ref, r_ref, acc_ref = head
    rows = slice(row0, row0 + nrows)
    groups = _sb_suffix_sums(_sb_logits(q_ref, k_ref, row0, nrows, k0, nkeys), uj, False)
    r, a = _sb_weights(r_ref[rows, :], groups)
    r_ref[rows, :] = r
    acc_ref[rows, :] += _dot(a, v_ref[pl.ds(k0, nkeys), :])


def _sb_band(heads, uj, d0, has_history):
    tiles = []
    for head in heads:
        for rt in range(TQ // TD):
            kts = [rt] + ([rt - 1] if rt > 0 or has_history else [])
            tiles.append((head, rt * TD, [pl.multiple_of(d0 + kt * TD, TD) for kt in kts]))
    logits = [[_sb_logits(head[0], head[1], row0, TD, k0, TD) for k0 in k0s]
              for head, row0, k0s in tiles]
    groups = [[_sb_suffix_sums(w, uj, n == 0) for n, w in enumerate(ws)] for ws in logits]
    for (head, row0, k0s), tile_groups in zip(tiles, groups):
        _, _, v_ref, r_ref, acc_ref = head
        r, acc = None, None
        for k0, g in zip(k0s, tile_groups):
            r, a = _sb_weights(r, g)
            pv = _dot(a, v_ref[pl.ds(k0, TD), :])
            acc = pv if acc is None else acc + pv
        r_ref[row0:row0 + TD, :] = r
        acc_ref[row0:row0 + TD, :] = acc


def _sb_kernel(q_ref, k_ref, v_ref, uj_ref, o_ref, acc_ref, r_ref):
    i = pl.program_id(2)
    uj = uj_ref[...]
    dh = acc_ref.shape[2]
    nt = TQ // TD
    d0 = pl.multiple_of(i * TQ, TQ)
    heads = []
    for hd in range(HEADS_PER_STEP):
        cols = slice(hd * dh, (hd + 1) * dh)
        heads.append((q_ref.at[:, cols], k_ref.at[:, cols], v_ref.at[:, cols],
                      r_ref.at[hd], acc_ref.at[hd]))

    @pl.when(i == 0)
    def _():
        _sb_band(heads, uj, d0, False)

    @pl.when(i > 0)
    def _():
        _sb_band(heads, uj, d0, True)

    for hd, head in enumerate(heads):
        def live():
            return jnp.max(r_ref[hd]) > DEAD_STICK_LOG2

        @pl.when(jnp.logical_and(i > 0, live()))
        def _():
            for rt in range(1, nt):
                for kt in reversed(range(-1, rt - 1)):
                    _sb_tile(head, uj, rt * TD, TD, pl.multiple_of(d0 + kt * TD, TD), TD)

            def cond(carry):
                kb, alive = carry
                return jnp.logical_and(kb >= 0, alive)

            def body(carry):
                kb, _ = carry
                _sb_tile(head, uj, 0, TQ, pl.multiple_of(kb * TD, TD), TD)
                return kb - 1, live()

            lax.while_loop(cond, body, (i * nt - 2, live()))

        o_ref[:, hd * dh:(hd + 1) * dh] = acc_ref[hd].astype(BF16)


def _suffix_sum_matrix():
    j = lax.broadcasted_iota(jnp.int32, (2 * LANES, 2 * LANES), 0) % LANES
    s = lax.broadcasted_iota(jnp.int32, (2 * LANES, 2 * LANES), 1)
    return jnp.where((s >= LANES) | (j > s), 1.0, 0.0).astype(BF16)


def _sb_attention(q, k, v, batch, seq):
    m, d = q.shape
    dh = d // N_HEADS
    dw = HEADS_PER_STEP * dh
    q3, k3, v3 = (t.reshape(batch, seq, d) for t in (q, k, v))
    kv_spec = pl.BlockSpec((None, seq, dw), lambda b, h, i: (b, 0, h))
    qo_spec = pl.BlockSpec((None, TQ, dw), lambda b, h, i: (b, i, h))
    o = pl.pallas_call(
        _sb_kernel,
        grid=(batch, N_HEADS // HEADS_PER_STEP, seq // TQ),
        in_specs=[qo_spec, kv_spec, kv_spec,
                  pl.BlockSpec((2 * LANES, 2 * LANES), lambda b, h, i: (0, 0))],
        out_specs=qo_spec,
        out_shape=jax.ShapeDtypeStruct((batch, seq, d), BF16),
        scratch_shapes=[pltpu.VMEM((HEADS_PER_STEP, TQ, dh), F32),
                        pltpu.VMEM((HEADS_PER_STEP, TQ, LANES), F32)],
        compiler_params=_params("arbitrary", "arbitrary", "arbitrary"),
        name="sb_attention",
    )(q3, k3, v3, _suffix_sum_matrix())
    return o.reshape(m, d)


def _proj_res_kernel(a_ref, w_ref, x_ref, o_ref):
    o_ref[...] = x_ref[...] + _dot(a_ref[...], w_ref[...])


def _proj_residual(a, w, layer, x):
    m, d = x.shape
    return pl.pallas_call(
        _proj_res_kernel,
        grid=(m // TM_RES,),
        in_specs=[
            pl.BlockSpec((TM_RES, d), lambda i: (i, 0)),
            pl.BlockSpec((None, d, d), lambda i: (layer, 0, 0)),
            pl.BlockSpec((TM_RES, d), lambda i: (i, 0)),
        ],
        out_specs=pl.BlockSpec((TM_RES, d), lambda i: (i, 0)),
        out_shape=jax.ShapeDtypeStruct((m, d), F32),
        compiler_params=_params("arbitrary"),
        name="proj_residual",
    )(a, w, x)


def _pool_kernel(x_ref, g_ref, w_ref, sc_ref, o_ref, buf_ref):
    j = pl.program_id(1)
    ts, d = x_ref.shape
    gw = d // len(POOL_WINDOWS)

    @pl.when(j == 0)
    def _():
        buf_ref[0:HALO, :] = jnp.zeros((HALO, d), F32)

    @pl.when(j > 0)
    def _():
        buf_ref[0:HALO, :] = buf_ref[ts:ts + HALO, :]

    x = x_ref[...]
    buf_ref[HALO:HALO + ts, :] = _rmsnorm(x, g_ref[...])
    pos1 = j * ts + 1 + lax.broadcasted_iota(jnp.int32, (ts, 1), 0)
    for gi, win in enumerate(POOL_WINDOWS):
        cols = slice(gi * gw, (gi + 1) * gw)
        cur = buf_ref[HALO:HALO + ts, cols]
        tot = cur
        for lag in range(1, win):
            tot = tot + buf_ref[HALO - lag:HALO - lag + ts, cols]
        cnt = jnp.minimum(pos1, win).astype(F32)
        piece = (tot / cnt - cur).astype(BF16)
        y = _dot(piece, w_ref[gi])
        o_ref[:, cols] = x[:, cols] + y * sc_ref[:, cols]


def _pool_mixer(x, g, w, layer, sc, batch, seq):
    m, d = x.shape
    ng = len(POOL_WINDOWS)
    gw = d // ng
    nt = seq // TS_POOL
    xspec = pl.BlockSpec((TS_POOL, d), lambda b, j: (b * nt + j, 0))
    vspec = pl.BlockSpec((1, d), lambda b, j: (0, 0))
    return pl.pallas_call(
        _pool_kernel,
        grid=(batch, nt),
        in_specs=[xspec, vspec,
                  pl.BlockSpec((None, ng, gw, gw), lambda b, j: (layer, 0, 0, 0)),
                  vspec],
        out_specs=xspec,
        out_shape=jax.ShapeDtypeStruct((m, d), F32),
        scratch_shapes=[pltpu.VMEM((HALO + TS_POOL, d), F32)],
        compiler_params=_params("arbitrary", "arbitrary"),
        name="pool_mixer",
    )(x, g, w, sc)


def _conv_in_kernel(tiles_per_seq, x_ref, g_ref, wb_ref, wc_ref, wu_ref, cw_ref,
                    o_ref, h_ref, carry_ref):
    i = pl.program_id(0)
    j = pl.program_id(1)

    @pl.when(i % tiles_per_seq == 0)
    def _():
        carry_ref[j] = jnp.zeros(carry_ref.shape[1:], F32)

    def gated_conv(h):
        cu = _dot(h, wc_ref[...]) * _dot(h, wu_ref[...])
        tm = cu.shape[0]
        prev = carry_ref[j]
        p1 = prev[SUBLANES - 1:SUBLANES, :]
        p2 = prev[SUBLANES - 2:SUBLANES - 1, :]
        rows = lax.broadcasted_iota(jnp.int32, cu.shape, 0)
        lag1 = jnp.where(rows == 0, p1, pltpu.roll(cu, 1, 0))
        lag2 = jnp.where(rows == 0, p2, jnp.where(rows == 1, p1, pltpu.roll(cu, 2, 0)))
        cw = cw_ref[...]
        y = cw[0:1, :] * lag2 + cw[1:2, :] * lag1 + cw[2:3, :] * cu
        carry_ref[j] = cu[tm - SUBLANES:tm, :]
        o_ref[...] = (_dot(h, wb_ref[...]) * y).astype(BF16)

    @pl.when(j == 0)
    def _():
        h = _rmsnorm(x_ref[...], g_ref[...]).astype(BF16)
        h_ref[...] = h
        gated_conv(h)

    @pl.when(j > 0)
    def _():
        gated_conv(h_ref[...])


def _conv_in(x, g, w3, layer, cw, seq):
    m, d = x.shape
    nj = d // TN_PROJ

    def wspec(part):
        return pl.BlockSpec((None, d, TN_PROJ),
                            lambda i, j: (layer, 0, part * nj + j))

    return pl.pallas_call(
        functools.partial(_conv_in_kernel, seq // TM_PROJ),
        grid=(m // TM_PROJ, nj),
        in_specs=[
            pl.BlockSpec((TM_PROJ, d), lambda i, j: (i, 0)),
            pl.BlockSpec((1, d), lambda i, j: (0, 0)),
            wspec(0), wspec(1), wspec(2),
            pl.BlockSpec((None, CONV_W, TN_PROJ), lambda i, j: (layer, 0, j)),
        ],
        out_specs=pl.BlockSpec((TM_PROJ, TN_PROJ), lambda i, j: (i, j)),
        out_shape=jax.ShapeDtypeStruct((m, d), BF16),
        scratch_shapes=[pltpu.VMEM((TM_PROJ, d), BF16),
                        pltpu.VMEM((nj, SUBLANES, TN_PROJ), F32)],
        compiler_params=_params("arbitrary", "arbitrary"),
        name="conv_in",
    )(x, g, w3, w3, w3, cw)


def _ffn_step(x_ref, g_ref, weights, o_ref, h_ref):
    def hidden_tile(h):
        wg, wu, wd = weights()
        gate = _dot(h, wg)
        up = _dot(h, wu)
        act = (gate * jax.nn.sigmoid(gate) * up).astype(BF16)
        return _dot(act, wd)

    @pl.when(pl.program_id(1) == 0)
    def _():
        x = x_ref[...]
        h = _rmsnorm(x, g_ref[...]).astype(BF16)
        h_ref[...] = h
        o_ref[...] = x + hidden_tile(h)

    @pl.when(pl.program_id(1) > 0)
    def _():
        o_ref[...] += hidden_tile(h_ref[...])


def _ffn_head_kernel(x_ref, g_ref, wg_ref, wu_ref, wd_ref,
                     o_ref, wg_out, wu_out, wd_out, h_ref):
    def weights():
        wg, wu, wd = (w[...].astype(BF16) for w in (wg_ref, wu_ref, wd_ref))
        wg_out[...], wu_out[...], wd_out[...] = wg, wu, wd
        return wg, wu, wd

    _ffn_step(x_ref, g_ref, weights, o_ref, h_ref)


def _ffn_kernel(x_ref, g_ref, wg_ref, wu_ref, wd_ref, head_ref, o_ref, h_ref):
    del head_ref
    _ffn_step(x_ref, g_ref, lambda: (wg_ref[...], wu_ref[...], wd_ref[...]), o_ref, h_ref)


def _ffn(x, g, wg32, wu32, wd32, layer):
    m, d = x.shape
    f = wg32.shape[2]
    once = pl.Buffered(1)
    y, wg, wu, wd = pl.pallas_call(
        _ffn_head_kernel,
        grid=(1, f // TF_HEAD),
        in_specs=[
            pl.BlockSpec((TM_FFN, d), lambda i, j: (0, 0), pipeline_mode=once),
            pl.BlockSpec((1, d), lambda i, j: (0, 0)),
            pl.BlockSpec((None, d, TF_HEAD), lambda i, j: (layer, 0, j)),
            pl.BlockSpec((None, d, TF_HEAD), lambda i, j: (layer, 0, j)),
            pl.BlockSpec((None, TF_HEAD, d), lambda i, j: (layer, j, 0)),
        ],
        out_specs=[
            pl.BlockSpec((TM_FFN, d), lambda i, j: (0, 0)),
            pl.BlockSpec((d, TF_HEAD), lambda i, j: (0, j)),
            pl.BlockSpec((d, TF_HEAD), lambda i, j: (0, j)),
            pl.BlockSpec((TF_HEAD, d), lambda i, j: (j, 0)),
        ],
        out_shape=[jax.ShapeDtypeStruct((m, d), F32),
                   jax.ShapeDtypeStruct((d, f), BF16),
                   jax.ShapeDtypeStruct((d, f), BF16),
                   jax.ShapeDtypeStruct((f, d), BF16)],
        scratch_shapes=[pltpu.VMEM((TM_FFN, d), BF16)],
        compiler_params=_params("arbitrary", "arbitrary"),
        name="ffn_head",
    )(x, g, wg32, wu32, wd32)
    return pl.pallas_call(
        _ffn_kernel,
        grid=(m // TM_FFN - 1, f // TF_FFN),
        in_specs=[
            pl.BlockSpec((TM_FFN, d), lambda i, j: (i + 1, 0)),
            pl.BlockSpec((1, d), lambda i, j: (0, 0)),
            pl.BlockSpec((d, TF_FFN), lambda i, j: (0, j)),
            pl.BlockSpec((d, TF_FFN), lambda i, j: (0, j)),
            pl.BlockSpec((TF_FFN, d), lambda i, j: (j, 0)),
            pl.BlockSpec(memory_space=pl.ANY),
        ],
        out_specs=pl.BlockSpec((TM_FFN, d), lambda i, j: (i + 1, 0)),
        out_shape=jax.ShapeDtypeStruct((m, d), F32),
        input_output_aliases={5: 0},
        scratch_shapes=[pltpu.VMEM((TM_FFN, d), BF16)],
        compiler_params=_params("arbitrary", "arbitrary"),
        name="ffn",
    )(x, g, wg, wu, wd, y)


def kernel(x, norm_mix_g, norm_ffn_g, sb_w_qkv, sb_g_q, sb_g_k, sb_w_o, pool_w, pool_scale, conv_w_in, conv_w, conv_w_out, ffn_w_gate, ffn_w_up, ffn_w_down):
    batch, seq, d = x.shape
    depth = norm_mix_g.shape[0]
    xf = x.reshape(batch * seq, d)
    sb_w_qkv, sb_w_o, pool_w, conv_w_in, conv_w_out = (
        w.astype(BF16) for w in (sb_w_qkv, sb_w_o, pool_w, conv_w_in, conv_w_out))
    for i in range(depth):
        kind, j = i % N_MIXERS, i // N_MIXERS
        g_mix = norm_mix_g[i:i + 1]
        if kind == 0:
            q, k, v = _qkv_proj(xf, g_mix, sb_w_qkv, j, sb_g_q[j:j + 1], sb_g_k[j:j + 1])
            o = _sb_attention(q, k, v, batch, seq)
            xf = _proj_residual(o, sb_w_o, j, xf)
        elif kind == 1:
            xf = _pool_mixer(xf, g_mix, pool_w, j, pool_scale[j:j + 1], batch, seq)
        else:
            gated = _conv_in(xf, g_mix, conv_w_in, j, conv_w, seq)
            xf = _proj_residual(gated, conv_w_out, j, xf)
        xf = _ffn(xf, norm_ffn_g[i:i + 1], ffn_w_gate, ffn_w_up, ffn_w_down, i)
    return xf.reshape(batch, seq, d)
```

```python
import functools

import jax
import jax.numpy as jnp
from jax import lax
from jax.experimental import pallas as pl
from jax.experimental.pallas import tpu as pltpu

F32 = jnp.float32
BF16 = jnp.bfloat16

EPS = 1e-6
LOG2E = 1.4426950408889634
DEAD_STICK_LOG2 = -160.0
N_HEADS = 16
POOL_WINDOWS = (2, 4, 8, 16)
CONV_W = 3
N_MIXERS = 3

LANES = 128
SUBLANES = 8
VMEM_LIMIT_BYTES = 56 * 1024 * 1024

TM_PROJ = 1024
TN_PROJ = 512
TM_RES = 1024
TM_FFN = 1024
TF_FFN = 512
TF_HEAD = 256
TS_POOL = 512
TQ = 1024
TD = 256
HEADS_PER_STEP = 2
HALO = 16
assert all(w & (w - 1) == 0 and w <= HALO for w in POOL_WINDOWS)


def _params(*sem):
    return pltpu.CompilerParams(dimension_semantics=sem,
                                vmem_limit_bytes=VMEM_LIMIT_BYTES)


def _rmsnorm(x, g):
    ms = jnp.mean(x * x, axis=-1, keepdims=True)
    return (x * lax.rsqrt(ms + EPS)) * g


def _dot(a, b):
    return jnp.dot(a, b, preferred_element_type=F32)


def _qkv_kernel(x_ref, g_ref, wq_ref, wk_ref, wv_ref, gq_ref, gk_ref,
                q_ref, k_ref, v_ref, h_ref):
    def head_norm(y, gh, out_ref):
        for c in range(y.shape[1] // LANES):
            yc = y[:, c * LANES:(c + 1) * LANES]
            ms = jnp.mean(yc * yc, axis=-1, keepdims=True)
            out_ref[:, c * LANES:(c + 1) * LANES] = (
                yc * lax.rsqrt(ms + EPS) * gh).astype(BF16)

    def project(h):
        head_norm(_dot(h, wq_ref[...]), gq_ref[...], q_ref)
        head_norm(_dot(h, wk_ref[...]), gk_ref[...], k_ref)
        v_ref[...] = _dot(h, wv_ref[...]).astype(BF16)

    @pl.when(pl.program_id(1) == 0)
    def _():
        h = _rmsnorm(x_ref[...], g_ref[...]).astype(BF16)
        h_ref[...] = h
        project(h)

    @pl.when(pl.program_id(1) > 0)
    def _():
        project(h_ref[...])


def _qkv_proj(x, g, w3, layer, gq, gk):
    m, d = x.shape
    nj = d // TN_PROJ
    out = jax.ShapeDtypeStruct((m, d), BF16)

    def wspec(part):
        return pl.BlockSpec((None, d, TN_PROJ),
                            lambda i, j: (layer, 0, part * nj + j))

    ospec = pl.BlockSpec((TM_PROJ, TN_PROJ), lambda i, j: (i, j))
    return pl.pallas_call(
        _qkv_kernel,
        grid=(m // TM_PROJ, nj),
        in_specs=[
            pl.BlockSpec((TM_PROJ, d), lambda i, j: (i, 0)),
            pl.BlockSpec((1, d), lambda i, j: (0, 0)),
            wspec(0), wspec(1), wspec(2),
            pl.BlockSpec((1, LANES), lambda i, j: (0, 0)),
            pl.BlockSpec((1, LANES), lambda i, j: (0, 0)),
        ],
        out_specs=[ospec, ospec, ospec],
        out_shape=[out, out, out],
        scratch_shapes=[pltpu.VMEM((TM_PROJ, d), BF16)],
        compiler_params=_params("arbitrary", "arbitrary"),
        name="qkv_proj",
    )(x, g, w3, w3, w3, gq, gk)


def _sb_logits(q_ref, k_ref, row0, nrows, k0, nkeys):
    q = q_ref[row0:row0 + nrows, :]
    kblk = k_ref[pl.ds(k0, nkeys), :]
    return lax.dot_general(q, kblk, (((1,), (1,)), ((), ())),
                           preferred_element_type=F32) * (q.shape[1] ** -0.5 * LOG2E)


def _sb_suffix_sums(w, uj, masked):
    groups = []
    for sb in range(w.shape[1] // LANES):
        ws = w[:, sb * LANES:(sb + 1) * LANES]
        lse = jnp.log2(1.0 + jnp.exp2(-jnp.abs(ws)))
        log_beta = jnp.minimum(ws, 0.0) - lse
        log_keep = log_beta - ws
        mask = None
        if masked:
            t_idx = lax.broadcasted_iota(jnp.int32, ws.shape, 0)
            s_idx = lax.broadcasted_iota(jnp.int32, ws.shape, 1) + sb * LANES
            mask = s_idx < t_idx
            log_keep = jnp.where(mask, log_keep, 0.0)
        hi = log_keep.astype(BF16)
        lo = (log_keep - hi.astype(F32)).astype(BF16)
        ct = _dot(jnp.concatenate([hi, lo], axis=1), uj)
        groups.append((log_beta, ct, mask))
    return groups


def _sb_weights(r, groups):
    parts = [None] * len(groups)
    for sb in reversed(range(len(groups))):
        log_beta, ct, mask = groups[sb]
        suffix, total = ct[:, :LANES], ct[:, LANES:]
        a = jnp.exp2(log_beta + (suffix if r is None else r + suffix))
        if mask is not None:
            a = jnp.where(mask, a, 0.0)
        parts[sb] = a.astype(BF16)
        r = total if r is None else r + total
    return r, jnp.concatenate(parts, axis=1)


def _sb_tile(head, uj, row0, nrows, k0, nkeys):
    q_ref, k_ref, v_ref, r_ref, acc_ref = head
    rows = slice(row0, row0 + nrows)
    groups = _sb_suffix_sums(_sb_logits(q_ref, k_ref, row0, nrows, k0, nkeys), uj, False)
    r, a = _sb_weights(r_ref[rows, :], groups)
    r_ref[rows, :] = r
    acc_ref[rows, :] += _dot(a, v_ref[pl.ds(k0, nkeys), :])


def _sb_band(heads, uj, d0, has_history):
    tiles = []
    for head in heads:
        for rt in range(TQ // TD):
            kts = [rt] + ([rt - 1] if rt > 0 or has_history else [])
            tiles.append((head, rt * TD, [pl.multiple_of(d0 + kt * TD, TD) for kt in kts]))
    logits = [[_sb_logits(head[0], head[1], row0, TD, k0, TD) for k0 in k0s]
              for head, row0, k0s in tiles]
    groups = [[_sb_suffix_sums(w, uj, n == 0) for n, w in enumerate(ws)] for ws in logits]
    for (head, row0, k0s), tile_groups in zip(tiles, groups):
        _, _, v_ref, r_ref, acc_ref = head
        r, acc = None, None
        for k0, g in zip(k0s, tile_groups):
            r, a = _sb_weights(r, g)
            pv = _dot(a, v_ref[pl.ds(k0, TD), :])
            acc = pv if acc is None else acc + pv
        r_ref[row0:row0 + TD, :] = r
        acc_ref[row0:row0 + TD, :] = acc


def _sb_kernel(q_ref, k_ref, v_ref, uj_ref, o_ref, acc_ref, r_ref):
    i = pl.program_id(2)
    uj = uj_ref[...]
    dh = acc_ref.shape[2]
    nt = TQ // TD
    d0 = pl.multiple_of(i * TQ, TQ)
    heads = []
    for hd in range(HEADS_PER_STEP):
        cols = slice(hd * dh, (hd + 1) * dh)
        heads.append((q_ref.at[:, cols], k_ref.at[:, cols], v_ref.at[:, cols],
                      r_ref.at[hd], acc_ref.at[hd]))

    @pl.when(i == 0)
    def _():
        _sb_band(heads, uj, d0, False)

    @pl.when(i > 0)
    def _():
        _sb_band(heads, uj, d0, True)

    for hd, head in enumerate(heads):
        def live():
            return jnp.max(r_ref[hd]) > DEAD_STICK_LOG2

        @pl.when(jnp.logical_and(i > 0, live()))
        def _():
            for rt in range(1, nt):
                for kt in reversed(range(-1, rt - 1)):
                    _sb_tile(head, uj, rt * TD, TD, pl.multiple_of(d0 + kt * TD, TD), TD)

            def cond(carry):
                kb, alive = carry
                return jnp.logical_and(kb >= 0, alive)

            def body(carry):
                kb, _ = carry
                _sb_tile(head, uj, 0, TQ, pl.multiple_of(kb * TD, TD), TD)
                return kb - 1, live()

            lax.while_loop(cond, body, (i * nt - 2, live()))

        o_ref[:, hd * dh:(hd + 1) * dh] = acc_ref[hd].astype(BF16)


def _suffix_sum_matrix():
    j = lax.broadcasted_iota(jnp.int32, (2 * LANES, 2 * LANES), 0) % LANES
    s = lax.broadcasted_iota(jnp.int32, (2 * LANES, 2 * LANES), 1)
    return jnp.where((s >= LANES) | (j > s), 1.0, 0.0).astype(BF16)


def _sb_attention(q, k, v, batch, seq):
    m, d = q.shape
    dh = d // N_HEADS
    dw = HEADS_PER_STEP * dh
    q3, k3, v3 = (t.reshape(batch, seq, d) for t in (q, k, v))
    kv_spec = pl.BlockSpec((None, seq, dw), lambda b, h, i: (b, 0, h))
    qo_spec = pl.BlockSpec((None, TQ, dw), lambda b, h, i: (b, i, h))
    o = pl.pallas_call(
        _sb_kernel,
        grid=(batch, N_HEADS // HEADS_PER_STEP, seq // TQ),
        in_specs=[qo_spec, kv_spec, kv_spec,
                  pl.BlockSpec((2 * LANES, 2 * LANES), lambda b, h, i: (0, 0))],
        out_specs=qo_spec,
        out_shape=jax.ShapeDtypeStruct((batch, seq, d), BF16),
        scratch_shapes=[pltpu.VMEM((HEADS_PER_STEP, TQ, dh), F32),
                        pltpu.VMEM((HEADS_PER_STEP, TQ, LANES), F32)],
        compiler_params=_params("arbitrary", "arbitrary", "arbitrary"),
        name="sb_attention",
    )(q3, k3, v3, _suffix_sum_matrix())
    return o.reshape(m, d)


def _proj_res_kernel(a_ref, w_ref, x_ref, o_ref):
    o_ref[...] = x_ref[...] + _dot(a_ref[...], w_ref[...])


def _proj_residual(a, w, layer, x):
    m, d = x.shape
    return pl.pallas_call(
        _proj_res_kernel,
        grid=(m // TM_RES,),
        in_specs=[
            pl.BlockSpec((TM_RES, d), lambda i: (i, 0)),
            pl.BlockSpec((None, d, d), lambda i: (layer, 0, 0),
                         pipeline_mode=pl.Buffered(1)),
            pl.BlockSpec((TM_RES, d), lambda i: (i, 0)),
        ],
        out_specs=pl.BlockSpec((TM_RES, d), lambda i: (i, 0)),
        out_shape=jax.ShapeDtypeStruct((m, d), F32),
        compiler_params=_params("arbitrary"),
        name="proj_residual",
    )(a, w, x)


def _pool_kernel(x_ref, g_ref, w_ref, sc_ref, o_ref, buf_ref):
    j = pl.program_id(1)
    ts, d = x_ref.shape
    gw = d // len(POOL_WINDOWS)

    @pl.when(j == 0)
    def _():
        buf_ref[0:HALO, :] = jnp.zeros((HALO, d), F32)

    @pl.when(j > 0)
    def _():
        buf_ref[0:HALO, :] = buf_ref[ts:ts + HALO, :]

    x = x_ref[...]
    buf_ref[HALO:HALO + ts, :] = _rmsnorm(x, g_ref[...])
    pos1 = j * ts + 1 + lax.broadcasted_iota(jnp.int32, (ts, 1), 0)
    for gi, win in enumerate(POOL_WINDOWS):
        cols = slice(gi * gw, (gi + 1) * gw)
        ext = buf_ref[:, cols]
        tot = ext
        lag = 1
        while lag < win:
            tot = tot + pltpu.roll(tot, lag, 0)
            lag *= 2
        tot = tot[HALO:, :]
        cur = ext[HALO:, :]
        cnt = jnp.minimum(pos1, win).astype(F32)
        piece = (tot / cnt - cur).astype(BF16)
        y = _dot(piece, w_ref[gi])
        o_ref[:, cols] = x[:, cols] + y * sc_ref[:, cols]


def _pool_mixer(x, g, w, layer, sc, batch, seq):
    m, d = x.shape
    ng = len(POOL_WINDOWS)
    gw = d // ng
    nt = seq // TS_POOL
    xspec = pl.BlockSpec((TS_POOL, d), lambda b, j: (b * nt + j, 0))
    vspec = pl.BlockSpec((1, d), lambda b, j: (0, 0))
    return pl.pallas_call(
        _pool_kernel,
        grid=(batch, nt),
        in_specs=[xspec, vspec,
                  pl.BlockSpec((None, ng, gw, gw), lambda b, j: (layer, 0, 0, 0)),
                  vspec],
        out_specs=xspec,
        out_shape=jax.ShapeDtypeStruct((m, d), F32),
        scratch_shapes=[pltpu.VMEM((HALO + TS_POOL, d), F32)],
        compiler_params=_params("arbitrary", "arbitrary"),
        name="pool_mixer",
    )(x, g, w, sc)


def _conv_in_kernel(tiles_per_seq, x_ref, g_ref, wb_ref, wc_ref, wu_ref, cw_ref,
                    o_ref, h_ref, carry_ref):
    i = pl.program_id(0)
    j = pl.program_id(1)

    @pl.when(i % tiles_per_seq == 0)
    def _():
        carry_ref[j] = jnp.zeros(carry_ref.shape[1:], F32)

    def gated_conv(h):
        cu = _dot(h, wc_ref[...]) * _dot(h, wu_ref[...])
        tm = cu.shape[0]
        prev = carry_ref[j]
        p1 = prev[SUBLANES - 1:SUBLANES, :]
        p2 = prev[SUBLANES - 2:SUBLANES - 1, :]
        rows = lax.broadcasted_iota(jnp.int32, cu.shape, 0)
        lag1 = jnp.where(rows == 0, p1, pltpu.roll(cu, 1, 0))
        lag2 = jnp.where(rows == 0, p2, jnp.where(rows == 1, p1, pltpu.roll(cu, 2, 0)))
        cw = cw_ref[...]
        y = cw[0:1, :] * lag2 + cw[1:2, :] * lag1 + cw[2:3, :] * cu
        carry_ref[j] = cu[tm - SUBLANES:tm, :]
        o_ref[...] = (_dot(h, wb_ref[...]) * y).astype(BF16)

    @pl.when(j == 0)
    def _():
        h = _rmsnorm(x_ref[...], g_ref[...]).astype(BF16)
        h_ref[...] = h
        gated_conv(h)

    @pl.when(j > 0)
    def _():
        gated_conv(h_ref[...])


def _conv_in(x, g, w3, layer, cw, seq):
    m, d = x.shape
    nj = d // TN_PROJ

    def wspec(part):
        return pl.BlockSpec((None, d, TN_PROJ),
                            lambda i, j: (layer, 0, part * nj + j))

    return pl.pallas_call(
        functools.partial(_conv_in_kernel, seq // TM_PROJ),
        grid=(m // TM_PROJ, nj),
        in_specs=[
            pl.BlockSpec((TM_PROJ, d), lambda i, j: (i, 0)),
            pl.BlockSpec((1, d), lambda i, j: (0, 0)),
            wspec(0), wspec(1), wspec(2),
            pl.BlockSpec((None, CONV_W, TN_PROJ), lambda i, j: (layer, 0, j)),
        ],
        out_specs=pl.BlockSpec((TM_PROJ, TN_PROJ), lambda i, j: (i, j)),
        out_shape=jax.ShapeDtypeStruct((m, d), BF16),
        scratch_shapes=[pltpu.VMEM((TM_PROJ, d), BF16),
                        pltpu.VMEM((nj, SUBLANES, TN_PROJ), F32)],
        compiler_params=_params("arbitrary", "arbitrary"),
        name="conv_in",
    )(x, g, w3, w3, w3, cw)


def _ffn_step(x_ref, g_ref, weights, o_ref, h_ref):
    def hidden_tile(h):
        wg, wu, wd = weights()
        gate = _dot(h, wg)
        up = _dot(h, wu)
        act = (gate * jax.nn.sigmoid(gate) * up).astype(BF16)
        return _dot(act, wd)

    @pl.when(pl.program_id(1) == 0)
    def _():
        x = x_ref[...]
        h = _rmsnorm(x, g_ref[...]).astype(BF16)
        h_ref[...] = h
        o_ref[...] = x + hidden_tile(h)

    @pl.when(pl.program_id(1) > 0)
    def _():
        o_ref[...] += hidden_tile(h_ref[...])


def _ffn_head_kernel(x_ref, g_ref, wg_ref, wu_ref, wd_ref,
                     o_ref, wg_out, wu_out, wd_out, h_ref):
    def weights():
        wg, wu, wd = (w[...].astype(BF16) for w in (wg_ref, wu_ref, wd_ref))
        wg_out[...], wu_out[...], wd_out[...] = wg, wu, wd
        return wg, wu, wd

    _ffn_step(x_ref, g_ref, weights, o_ref, h_ref)


def _ffn_kernel(x_ref, g_ref, wg_ref, wu_ref, wd_ref, head_ref, o_ref, h_ref):
    del head_ref
    _ffn_step(x_ref, g_ref, lambda: (wg_ref[...], wu_ref[...], wd_ref[...]), o_ref, h_ref)


def _ffn(x, g, wg32, wu32, wd32, layer):
    m, d = x.shape
    f = wg32.shape[2]
    once = pl.Buffered(1)
    y, wg, wu, wd = pl.pallas_call(
        _ffn_head_kernel,
        grid=(1, f // TF_HEAD),
        in_specs=[
            pl.BlockSpec((TM_FFN, d), lambda i, j: (0, 0), pipeline_mode=once),
            pl.BlockSpec((1, d), lambda i, j: (0, 0)),
            pl.BlockSpec((None, d, TF_HEAD), lambda i, j: (layer, 0, j)),
            pl.BlockSpec((None, d, TF_HEAD), lambda i, j: (layer, 0, j)),
            pl.BlockSpec((None, TF_HEAD, d), lambda i, j: (layer, j, 0)),
        ],
        out_specs=[
            pl.BlockSpec((TM_FFN, d), lambda i, j: (0, 0)),
            pl.BlockSpec((d, TF_HEAD), lambda i, j: (0, j)),
            pl.BlockSpec((d, TF_HEAD), lambda i, j: (0, j)),
            pl.BlockSpec((TF_HEAD, d), lambda i, j: (j, 0)),
        ],
        out_shape=[jax.ShapeDtypeStruct((m, d), F32),
                   jax.ShapeDtypeStruct((d, f), BF16),
                   jax.ShapeDtypeStruct((d, f), BF16),
                   jax.ShapeDtypeStruct((f, d), BF16)],
        scratch_shapes=[pltpu.VMEM((TM_FFN, d), BF16)],
        compiler_params=_params("arbitrary", "arbitrary"),
        name="ffn_head",
    )(x, g, wg32, wu32, wd32)
    return pl.pallas_call(
        _ffn_kernel,
        grid=(m // TM_FFN - 1, f // TF_FFN),
        in_specs=[
            pl.BlockSpec((TM_FFN, d), lambda i, j: (i + 1, 0)),
            pl.BlockSpec((1, d), lambda i, j: (0, 0)),
            pl.BlockSpec((d, TF_FFN), lambda i, j: (0, j)),
            pl.BlockSpec((d, TF_FFN), lambda i, j: (0, j)),
            pl.BlockSpec((TF_FFN, d), lambda i, j: (j, 0)),
            pl.BlockSpec(memory_space=pl.ANY),
        ],
        out_specs=pl.BlockSpec((TM_FFN, d), lambda i, j: (i + 1, 0)),
        out_shape=jax.ShapeDtypeStruct((m, d), F32),
        input_output_aliases={5: 0},
        scratch_shapes=[pltpu.VMEM((TM_FFN, d), BF16)],
        compiler_params=_params("arbitrary", "arbitrary"),
        name="ffn",
    )(x, g, wg, wu, wd, y)


def kernel(x, norm_mix_g, norm_ffn_g, sb_w_qkv, sb_g_q, sb_g_k, sb_w_o, pool_w, pool_scale, conv_w_in, conv_w, conv_w_out, ffn_w_gate, ffn_w_up, ffn_w_down):
    batch, seq, d = x.shape
    depth = norm_mix_g.shape[0]
    xf = x.reshape(batch * seq, d)
    sb_w_qkv, sb_w_o, pool_w, conv_w_in, conv_w_out = (
        w.astype(BF16) for w in (sb_w_qkv, sb_w_o, pool_w, conv_w_in, conv_w_out))
    for i in range(depth):
        kind, j = i % N_MIXERS, i // N_MIXERS
        g_mix = norm_mix_g[i:i + 1]
        if kind == 0:
            q, k, v = _qkv_proj(xf, g_mix, sb_w_qkv, j, sb_g_q[j:j + 1], sb_g_k[j:j + 1])
            o = _sb_attention(q, k, v, batch, seq)
            xf = _proj_residual(o, sb_w_o, j, xf)
        elif kind == 1:
            xf = _pool_mixer(xf, g_mix, pool_w, j, pool_scale[j:j + 1], batch, seq)
        else:
            gated = _conv_in(xf, g_mix, conv_w_in, j, conv_w, seq)
            xf = _proj_residual(gated, conv_w_out, j, xf)
        xf = _ffn(xf, norm_ffn_g[i:i + 1], ffn_w_gate, ffn_w_up, ffn_w_down, i)
    return xf.reshape(batch, seq, d)
```

```python
import functools

import jax
import jax.numpy as jnp
from jax import lax
from jax.experimental import pallas as pl
from jax.experimental.pallas import tpu as pltpu

F32 = jnp.float32
BF16 = jnp.bfloat16

EPS = 1e-6
LOG2E = 1.4426950408889634
DEAD_STICK_LOG2 = -160.0
N_HEADS = 16
POOL_WINDOWS = (2, 4, 8, 16)
CONV_W = 3
N_MIXERS = 3

LANES = 128
SUBLANES = 8
VMEM_LIMIT_BYTES = 56 * 1024 * 1024

TM_PROJ = 1024
TN_PROJ = 512
TM_RES = 512
TM_FFN = 1024
TF_FFN = 512
TF_HEAD = 256
TS_POOL = 512
TQ = 1024
TD = 256
HEADS_PER_STEP = 4
HALO = 16
assert all(w & (w - 1) == 0 and w <= HALO for w in POOL_WINDOWS)


def _params(*sem):
    return pltpu.CompilerParams(dimension_semantics=sem,
                                vmem_limit_bytes=VMEM_LIMIT_BYTES)


def _rmsnorm(x, g):
    ms = jnp.mean(x * x, axis=-1, keepdims=True)
    return (x * lax.rsqrt(ms + EPS)) * g


def _dot(a, b):
    return jnp.dot(a, b, preferred_element_type=F32)


def _qkv_kernel(x_ref, g_ref, wq_ref, wk_ref, wv_ref, gq_ref, gk_ref,
                q_ref, k_ref, v_ref, h_ref):
    def head_norm(y, gh, out_ref):
        for c in range(y.shape[1] // LANES):
            yc = y[:, c * LANES:(c + 1) * LANES]
            ms = jnp.mean(yc * yc, axis=-1, keepdims=True)
            out_ref[:, c * LANES:(c + 1) * LANES] = (
                yc * lax.rsqrt(ms + EPS) * gh).astype(BF16)

    def project(h):
        head_norm(_dot(h, wq_ref[...]), gq_ref[...], q_ref)
        head_norm(_dot(h, wk_ref[...]), gk_ref[...], k_ref)
        v_ref[...] = _dot(h, wv_ref[...]).astype(BF16)

    @pl.when(pl.program_id(1) == 0)
    def _():
        h = _rmsnorm(x_ref[...], g_ref[...]).astype(BF16)
        h_ref[...] = h
        project(h)

    @pl.when(pl.program_id(1) > 0)
    def _():
        project(h_ref[...])


def _qkv_proj(x, g, w3, layer, gq, gk):
    m, d = x.shape
    nj = d // TN_PROJ
    out = jax.ShapeDtypeStruct((m, d), BF16)

    def wspec(part):
        return pl.BlockSpec((None, d, TN_PROJ),
                            lambda i, j: (layer, 0, part * nj + j))

    ospec = pl.BlockSpec((TM_PROJ, TN_PROJ), lambda i, j: (i, j))
    return pl.pallas_call(
        _qkv_kernel,
        grid=(m // TM_PROJ, nj),
        in_specs=[
            pl.BlockSpec((TM_PROJ, d), lambda i, j: (i, 0)),
            pl.BlockSpec((1, d), lambda i, j: (0, 0)),
            wspec(0), wspec(1), wspec(2),
            pl.BlockSpec((1, LANES), lambda i, j: (0, 0)),
            pl.BlockSpec((1, LANES), lambda i, j: (0, 0)),
        ],
        out_specs=[ospec, ospec, ospec],
        out_shape=[out, out, out],
        scratch_shapes=[pltpu.VMEM((TM_PROJ, d), BF16)],
        compiler_params=_params("arbitrary", "arbitrary"),
        name="qkv_proj",
    )(x, g, w3, w3, w3, gq, gk)


def _sb_logits(q_ref, k_ref, row0, nrows, k0, nkeys):
    q = q_ref[row0:row0 + nrows, :]
    kblk = k_ref[pl.ds(k0, nkeys), :]
    return lax.dot_general(q, kblk, (((1,), (1,)), ((), ())),
                           preferred_element_type=F32) * (q.shape[1] ** -0.5 * LOG2E)


def _sb_suffix_sums(w, uj, masked):
    groups = []
    for sb in range(w.shape[1] // LANES):
        ws = w[:, sb * LANES:(sb + 1) * LANES]
        lse = jnp.log2(1.0 + jnp.exp2(-jnp.abs(ws)))
        log_beta = jnp.minimum(ws, 0.0) - lse
        log_keep = log_beta - ws
        mask = None
        if masked:
            t_idx = lax.broadcasted_iota(jnp.int32, ws.shape, 0)
            s_idx = lax.broadcasted_iota(jnp.int32, ws.shape, 1) + sb * LANES
            mask = s_idx < t_idx
            log_keep = jnp.where(mask, log_keep, 0.0)
        hi = log_keep.astype(BF16)
        lo = (log_keep - hi.astype(F32)).astype(BF16)
        ct = _dot(jnp.concatenate([hi, lo], axis=1), uj)
        groups.append((log_beta, ct, mask))
    return groups


def _sb_weights(r, groups):
    parts = [None] * len(groups)
    for sb in reversed(range(len(groups))):
        log_beta, ct, mask = groups[sb]
        suffix, total = ct[:, :LANES], ct[:, LANES:]
        a = jnp.exp2(log_beta + (suffix if r is None else r + suffix))
        if mask is not None:
            a = jnp.where(mask, a, 0.0)
        parts[sb] = a.astype(BF16)
        r = total if r is None else r + total
    return r, jnp.concatenate(parts, axis=1)


def _sb_tile(head, uj, row0, nrows, k0, nkeys):
    q_ref, k_ref, v_ref, r_ref, acc_ref = head
    rows = slice(row0, row0 + nrows)
    groups = _sb_suffix_sums(_sb_logits(q_ref, k_ref, row0, nrows, k0, nkeys), uj, False)
    r, a = _sb_weights(r_ref[rows, :], groups)
    r_ref[rows, :] = r
    acc_ref[rows, :] += _dot(a, v_ref[pl.ds(k0, nkeys), :])


def _sb_band(heads, uj, d0, has_history):
    tiles = []
    for head in heads:
        for rt in range(TQ // TD):
            kts = [rt] + ([rt - 1] if rt > 0 or has_history else [])
            tiles.append((head, rt * TD, [pl.multiple_of(d0 + kt * TD, TD) for kt in kts]))
    logits = [[_sb_logits(head[0], head[1], row0, TD, k0, TD) for k0 in k0s]
              for head, row0, k0s in tiles]
    groups = [[_sb_suffix_sums(w, uj, n == 0) for n, w in enumerate(ws)] for ws in logits]
    for (head, row0, k0s), tile_groups in zip(tiles, groups):
        _, _, v_ref, r_ref, acc_ref = head
        r, acc = None, None
        for k0, g in zip(k0s, tile_groups):
            r, a = _sb_weights(r, g)
            pv = _dot(a, v_ref[pl.ds(k0, TD), :])
            acc = pv if acc is None else acc + pv
        r_ref[row0:row0 + TD, :] = r
        acc_ref[row0:row0 + TD, :] = acc


def _sb_kernel(q_ref, k_ref, v_ref, uj_ref, o_ref, acc_ref, r_ref):
    i = pl.program_id(2)
    uj = uj_ref[...]
    dh = acc_ref.shape[2]
    nt = TQ // TD
    d0 = pl.multiple_of(i * TQ, TQ)
    heads = []
    for hd in range(HEADS_PER_STEP):
        cols = slice(hd * dh, (hd + 1) * dh)
        heads.append((q_ref.at[:, cols], k_ref.at[:, cols], v_ref.at[:, cols],
                      r_ref.at[hd], acc_ref.at[hd]))

    @pl.when(i == 0)
    def _():
        _sb_band(heads, uj, d0, False)

    @pl.when(i > 0)
    def _():
        _sb_band(heads, uj, d0, True)

    for hd, head in enumerate(heads):
        def live():
            return jnp.max(r_ref[hd]) > DEAD_STICK_LOG2

        @pl.when(jnp.logical_and(i > 0, live()))
        def _():
            for rt in range(1, nt):
                for kt in reversed(range(-1, rt - 1)):
                    _sb_tile(head, uj, rt * TD, TD, pl.multiple_of(d0 + kt * TD, TD), TD)

            def cond(carry):
                kb, alive = carry
                return jnp.logical_and(kb >= 0, alive)

            def body(carry):
                kb, _ = carry
                _sb_tile(head, uj, 0, TQ, pl.multiple_of(kb * TD, TD), TD)
                return kb - 1, live()

            lax.while_loop(cond, body, (i * nt - 2, live()))

        o_ref[:, hd * dh:(hd + 1) * dh] = acc_ref[hd].astype(BF16)


def _suffix_sum_matrix():
    j = lax.broadcasted_iota(jnp.int32, (2 * LANES, 2 * LANES), 0) % LANES
    s = lax.broadcasted_iota(jnp.int32, (2 * LANES, 2 * LANES), 1)
    return jnp.where((s >= LANES) | (j > s), 1.0, 0.0).astype(BF16)


def _sb_attention(q, k, v, batch, seq):
    m, d = q.shape
    dh = d // N_HEADS
    dw = HEADS_PER_STEP * dh
    q3, k3, v3 = (t.reshape(batch, seq, d) for t in (q, k, v))
    kv_spec = pl.BlockSpec((None, seq, dw), lambda b, h, i: (b, 0, h))
    qo_spec = pl.BlockSpec((None, TQ, dw), lambda b, h, i: (b, i, h))
    o = pl.pallas_call(
        _sb_kernel,
        grid=(batch, N_HEADS // HEADS_PER_STEP, seq // TQ),
        in_specs=[qo_spec, kv_spec, kv_spec,
                  pl.BlockSpec((2 * LANES, 2 * LANES), lambda b, h, i: (0, 0))],
        out_specs=qo_spec,
        out_shape=jax.ShapeDtypeStruct((batch, seq, d), BF16),
        scratch_shapes=[pltpu.VMEM((HEADS_PER_STEP, TQ, dh), F32),
                        pltpu.VMEM((HEADS_PER_STEP, TQ, LANES), F32)],
        compiler_params=_params("arbitrary", "arbitrary", "arbitrary"),
        name="sb_attention",
    )(q3, k3, v3, _suffix_sum_matrix())
    return o.reshape(m, d)


def _proj_res_kernel(a_ref, w_ref, x_ref, o_ref):
    o_ref[...] = x_ref[...] + _dot(a_ref[...], w_ref[...])


def _proj_residual(a, w, layer, x):
    m, d = x.shape
    return pl.pallas_call(
        _proj_res_kernel,
        grid=(m // TM_RES,),
        in_specs=[
            pl.BlockSpec((TM_RES, d), lambda i: (i, 0)),
            pl.BlockSpec((None, d, d), lambda i: (layer, 0, 0)),
            pl.BlockSpec((TM_RES, d), lambda i: (i, 0)),
        ],
        out_specs=pl.BlockSpec((TM_RES, d), lambda i: (i, 0)),
        out_shape=jax.ShapeDtypeStruct((m, d), F32),
        compiler_params=_params("arbitrary"),
        name="proj_residual",
    )(a, w, x)


def _pool_kernel(x_ref, g_ref, w_ref, sc_ref, o_ref, buf_ref):
    j = pl.program_id(1)
    ts, d = x_ref.shape
    gw = d // len(POOL_WINDOWS)

    @pl.when(j == 0)
    def _():
        buf_ref[0:HALO, :] = jnp.zeros((HALO, d), F32)

    @pl.when(j > 0)
    def _():
        buf_ref[0:HALO, :] = buf_ref[ts:ts + HALO, :]

    x = x_ref[...]
    buf_ref[HALO:HALO + ts, :] = _rmsnorm(x, g_ref[...])
    pos1 = j * ts + 1 + lax.broadcasted_iota(jnp.int32, (ts, 1), 0)
    for gi, win in enumerate(POOL_WINDOWS):
        cols = slice(gi * gw, (gi + 1) * gw)
        ext = buf_ref[:, cols]
        tot = ext
        lag = 1
        while lag < win:
            tot = tot + pltpu.roll(tot, lag, 0)
            lag *= 2
        tot = tot[HALO:, :]
        cur = ext[HALO:, :]
        cnt = jnp.minimum(pos1, win).astype(F32)
        piece = (tot / cnt - cur).astype(BF16)
        y = _dot(piece, w_ref[gi])
        o_ref[:, cols] = x[:, cols] + y * sc_ref[:, cols]


def _pool_mixer(x, g, w, layer, sc, batch, seq):
    m, d = x.shape
    ng = len(POOL_WINDOWS)
    gw = d // ng
    nt = seq // TS_POOL
    xspec = pl.BlockSpec((TS_POOL, d), lambda b, j: (b * nt + j, 0))
    vspec = pl.BlockSpec((1, d), lambda b, j: (0, 0))
    return pl.pallas_call(
        _pool_kernel,
        grid=(batch, nt),
        in_specs=[xspec, vspec,
                  pl.BlockSpec((None, ng, gw, gw), lambda b, j: (layer, 0, 0, 0)),
                  vspec],
        out_specs=xspec,
        out_shape=jax.ShapeDtypeStruct((m, d), F32),
        scratch_shapes=[pltpu.VMEM((HALO + TS_POOL, d), F32)],
        compiler_params=_params("arbitrary", "arbitrary"),
        name="pool_mixer",
    )(x, g, w, sc)


def _conv_in_kernel(tiles_per_seq, x_ref, g_ref, wb_ref, wc_ref, wu_ref, cw_ref,
                    o_ref, h_ref, carry_ref):
    i = pl.program_id(0)
    j = pl.program_id(1)

    @pl.when(i % tiles_per_seq == 0)
    def _():
        carry_ref[j] = jnp.zeros(carry_ref.shape[1:], F32)

    def gated_conv(h):
        cu = _dot(h, wc_ref[...]) * _dot(h, wu_ref[...])
        tm = cu.shape[0]
        prev = carry_ref[j]
        p1 = prev[SUBLANES - 1:SUBLANES, :]
        p2 = prev[SUBLANES - 2:SUBLANES - 1, :]
        rows = lax.broadcasted_iota(jnp.int32, cu.shape, 0)
        lag1 = jnp.where(rows == 0, p1, pltpu.roll(cu, 1, 0))
        lag2 = jnp.where(rows == 0, p2, jnp.where(rows == 1, p1, pltpu.roll(cu, 2, 0)))
        cw = cw_ref[...]
        y = cw[0:1, :] * lag2 + cw[1:2, :] * lag1 + cw[2:3, :] * cu
        carry_ref[j] = cu[tm - SUBLANES:tm, :]
        o_ref[...] = (_dot(h, wb_ref[...]) * y).astype(BF16)

    @pl.when(j == 0)
    def _():
        h = _rmsnorm(x_ref[...], g_ref[...]).astype(BF16)
        h_ref[...] = h
        gated_conv(h)

    @pl.when(j > 0)
    def _():
        gated_conv(h_ref[...])


def _conv_in(x, g, w3, layer, cw, seq):
    m, d = x.shape
    nj = d // TN_PROJ

    def wspec(part):
        return pl.BlockSpec((None, d, TN_PROJ),
                            lambda i, j: (layer, 0, part * nj + j))

    return pl.pallas_call(
        functools.partial(_conv_in_kernel, seq // TM_PROJ),
        grid=(m // TM_PROJ, nj),
        in_specs=[
            pl.BlockSpec((TM_PROJ, d), lambda i, j: (i, 0)),
            pl.BlockSpec((1, d), lambda i, j: (0, 0)),
            wspec(0), wspec(1), wspec(2),
            pl.BlockSpec((None, CONV_W, TN_PROJ), lambda i, j: (layer, 0, j)),
        ],
        out_specs=pl.BlockSpec((TM_PROJ, TN_PROJ), lambda i, j: (i, j)),
        out_shape=jax.ShapeDtypeStruct((m, d), BF16),
        scratch_shapes=[pltpu.VMEM((TM_PROJ, d), BF16),
                        pltpu.VMEM((nj, SUBLANES, TN_PROJ), F32)],
        compiler_params=_params("arbitrary", "arbitrary"),
        name="conv_in",
    )(x, g, w3, w3, w3, cw)


def _ffn_step(x_ref, g_ref, weights, o_ref, h_ref):
    def hidden_tile(h):
        wg, wu, wd = weights()
        gate = _dot(h, wg)
        up = _dot(h, wu)
        act = (gate * jax.nn.sigmoid(gate) * up).astype(BF16)
        return _dot(act, wd)

    @pl.when(pl.program_id(1) == 0)
    def _():
        x = x_ref[...]
        h = _rmsnorm(x, g_ref[...]).astype(BF16)
        h_ref[...] = h
        o_ref[...] = x + hidden_tile(h)

    @pl.when(pl.program_id(1) > 0)
    def _():
        o_ref[...] += hidden_tile(h_ref[...])


def _ffn_head_kernel(x_ref, g_ref, wg_ref, wu_ref, wd_ref,
                     o_ref, wg_out, wu_out, wd_out, h_ref):
    def weights():
        wg, wu, wd = (w[...].astype(BF16) for w in (wg_ref, wu_ref, wd_ref))
        wg_out[...], wu_out[...], wd_out[...] = wg, wu, wd
        return wg, wu, wd

    _ffn_step(x_ref, g_ref, weights, o_ref, h_ref)


def _ffn_kernel(x_ref, g_ref, wg_ref, wu_ref, wd_ref, head_ref, o_ref, h_ref):
    del head_ref
    _ffn_step(x_ref, g_ref, lambda: (wg_ref[...], wu_ref[...], wd_ref[...]), o_ref, h_ref)


def _ffn(x, g, wg32, wu32, wd32, layer):
    m, d = x.shape
    f = wg32.shape[2]
    once = pl.Buffered(1)
    y, wg, wu, wd = pl.pallas_call(
        _ffn_head_kernel,
        grid=(1, f // TF_HEAD),
        in_specs=[
            pl.BlockSpec((TM_FFN, d), lambda i, j: (0, 0), pipeline_mode=once),
            pl.BlockSpec((1, d), lambda i, j: (0, 0)),
            pl.BlockSpec((None, d, TF_HEAD), lambda i, j: (layer, 0, j)),
            pl.BlockSpec((None, d, TF_HEAD), lambda i, j: (layer, 0, j)),
            pl.BlockSpec((None, TF_HEAD, d), lambda i, j: (layer, j, 0)),
        ],
        out_specs=[
            pl.BlockSpec((TM_FFN, d), lambda i, j: (0, 0)),
            pl.BlockSpec((d, TF_HEAD), lambda i, j: (0, j)),
            pl.BlockSpec((d, TF_HEAD), lambda i, j: (0, j)),
            pl.BlockSpec((TF_HEAD, d), lambda i, j: (j, 0)),
        ],
        out_shape=[jax.ShapeDtypeStruct((m, d), F32),
                   jax.ShapeDtypeStruct((d, f), BF16),
                   jax.ShapeDtypeStruct((d, f), BF16),
                   jax.ShapeDtypeStruct((f, d), BF16)],
        scratch_shapes=[pltpu.VMEM((TM_FFN, d), BF16)],
        compiler_params=_params("arbitrary", "arbitrary"),
        name="ffn_head",
    )(x, g, wg32, wu32, wd32)
    return pl.pallas_call(
        _ffn_kernel,
        grid=(m // TM_FFN - 1, f // TF_FFN),
        in_specs=[
            pl.BlockSpec((TM_FFN, d), lambda i, j: (i + 1, 0)),
            pl.BlockSpec((1, d), lambda i, j: (0, 0)),
            pl.BlockSpec((d, TF_FFN), lambda i, j: (0, j)),
            pl.BlockSpec((d, TF_FFN), lambda i, j: (0, j)),
            pl.BlockSpec((TF_FFN, d), lambda i, j: (j, 0)),
            pl.BlockSpec(memory_space=pl.ANY),
        ],
        out_specs=pl.BlockSpec((TM_FFN, d), lambda i, j: (i + 1, 0)),
        out_shape=jax.ShapeDtypeStruct((m, d), F32),
        input_output_aliases={5: 0},
        scratch_shapes=[pltpu.VMEM((TM_FFN, d), BF16)],
        compiler_params=_params("arbitrary", "arbitrary"),
        name="ffn",
    )(x, g, wg, wu, wd, y)


def kernel(x, norm_mix_g, norm_ffn_g, sb_w_qkv, sb_g_q, sb_g_k, sb_w_o, pool_w, pool_scale, conv_w_in, conv_w, conv_w_out, ffn_w_gate, ffn_w_up, ffn_w_down):
    batch, seq, d = x.shape
    depth = norm_mix_g.shape[0]
    xf = x.reshape(batch * seq, d)
    sb_w_qkv, sb_w_o, pool_w, conv_w_in, conv_w_out = (
        w.astype(BF16) for w in (sb_w_qkv, sb_w_o, pool_w, conv_w_in, conv_w_out))
    for i in range(depth):
        kind, j = i % N_MIXERS, i // N_MIXERS
        g_mix = norm_mix_g[i:i + 1]
        if kind == 0:
            q, k, v = _qkv_proj(xf, g_mix, sb_w_qkv, j, sb_g_q[j:j + 1], sb_g_k[j:j + 1])
            o = _sb_attention(q, k, v, batch, seq)
            xf = _proj_residual(o, sb_w_o, j, xf)
        elif kind == 1:
            xf = _pool_mixer(xf, g_mix, pool_w, j, pool_scale[j:j + 1], batch, seq)
        else:
            gated = _conv_in(xf, g_mix, conv_w_in, j, conv_w, seq)
            xf = _proj_residual(gated, conv_w_out, j, xf)
        xf = _ffn(xf, norm_ffn_g[i:i + 1], ffn_w_gate, ffn_w_up, ffn_w_down, i)
    return xf.reshape(batch, seq, d)
```

```python
import functools

import jax
import jax.numpy as jnp
from jax import lax
from jax.experimental import pallas as pl
from jax.experimental.pallas import tpu as pltpu

F32 = jnp.float32
BF16 = jnp.bfloat16

EPS = 1e-6
LOG2E = 1.4426950408889634
DEAD_STICK_LOG2 = -160.0
N_HEADS = 16
POOL_WINDOWS = (2, 4, 8, 16)
CONV_W = 3
N_MIXERS = 3

LANES = 128
SUBLANES = 8
VMEM_LIMIT_BYTES = 56 * 1024 * 1024

TM_PROJ = 1024
TN_PROJ = 512
TM_RES = 512
TM_FFN = 1024
TF_FFN = 512
TF_HEAD = 256
TS_POOL = 512
TQ = 1024
TD = 256
HEADS_PER_STEP = 4
HALO = 16
assert all(w & (w - 1) == 0 and w <= HALO for w in POOL_WINDOWS)


def _params(*sem):
    return pltpu.CompilerParams(dimension_semantics=sem,
                                vmem_limit_bytes=VMEM_LIMIT_BYTES)


def _rmsnorm(x, g):
    ms = jnp.mean(x * x, axis=-1, keepdims=True)
    return (x * lax.rsqrt(ms + EPS)) * g


def _dot(a, b):
    return jnp.dot(a, b, preferred_element_type=F32)


def _qkv_kernel(x_ref, g_ref, wq_ref, wk_ref, wv_ref, gq_ref, gk_ref,
                q_ref, k_ref, v_ref, h_ref):
    def head_norm(y, gh, out_ref):
        for c in range(y.shape[1] // LANES):
            yc = y[:, c * LANES:(c + 1) * LANES]
            ms = jnp.mean(yc * yc, axis=-1, keepdims=True)
            out_ref[:, c * LANES:(c + 1) * LANES] = (
                yc * lax.rsqrt(ms + EPS) * gh).astype(BF16)

    def project(h):
        head_norm(_dot(h, wq_ref[...]), gq_ref[...], q_ref)
        head_norm(_dot(h, wk_ref[...]), gk_ref[...], k_ref)
        v_ref[...] = _dot(h, wv_ref[...]).astype(BF16)

    @pl.when(pl.program_id(1) == 0)
    def _():
        h = _rmsnorm(x_ref[...], g_ref[...]).astype(BF16)
        h_ref[...] = h
        project(h)

    @pl.when(pl.program_id(1) > 0)
    def _():
        project(h_ref[...])


def _qkv_proj(x, g, w3, layer, gq, gk):
    m, d = x.shape
    nj = d // TN_PROJ
    out = jax.ShapeDtypeStruct((m, d), BF16)

    def wspec(part):
        return pl.BlockSpec((None, d, TN_PROJ),
                            lambda i, j: (layer, 0, part * nj + j))

    ospec = pl.BlockSpec((TM_PROJ, TN_PROJ), lambda i, j: (i, j))
    return pl.pallas_call(
        _qkv_kernel,
        grid=(m // TM_PROJ, nj),
        in_specs=[
            pl.BlockSpec((TM_PROJ, d), lambda i, j: (i, 0)),
            pl.BlockSpec((1, d), lambda i, j: (0, 0)),
            wspec(0), wspec(1), wspec(2),
            pl.BlockSpec((1, LANES), lambda i, j: (0, 0)),
            pl.BlockSpec((1, LANES), lambda i, j: (0, 0)),
        ],
        out_specs=[ospec, ospec, ospec],
        out_shape=[out, out, out],
        scratch_shapes=[pltpu.VMEM((TM_PROJ, d), BF16)],
        compiler_params=_params("arbitrary", "arbitrary"),
        name="qkv_proj",
    )(x, g, w3, w3, w3, gq, gk)


def _sb_logits(q_ref, k_ref, row0, nrows, k0, nkeys):
    q = q_ref[row0:row0 + nrows, :]
    kblk = k_ref[pl.ds(k0, nkeys), :]
    return lax.dot_general(q, kblk, (((1,), (1,)), ((), ())),
                           preferred_element_type=F32) * (q.shape[1] ** -0.5 * LOG2E)


def _sb_suffix_sums(w, uj, masked):
    groups = []
    for sb in range(w.shape[1] // LANES):
        ws = w[:, sb * LANES:(sb + 1) * LANES]
        lse = jnp.log2(1.0 + jnp.exp2(-jnp.abs(ws)))
        log_beta = jnp.minimum(ws, 0.0) - lse
        log_keep = log_beta - ws
        mask = None
        if masked:
            t_idx = lax.broadcasted_iota(jnp.int32, ws.shape, 0)
            s_idx = lax.broadcasted_iota(jnp.int32, ws.shape, 1) + sb * LANES
            mask = s_idx < t_idx
            log_keep = jnp.where(mask, log_keep, 0.0)
        hi = log_keep.astype(BF16)
        lo = (log_keep - hi.astype(F32)).astype(BF16)
        ct = _dot(jnp.concatenate([hi, lo], axis=1), uj)
        groups.append((log_beta, ct, mask))
    return groups


def _sb_weights(r, groups):
    parts = [None] * len(groups)
    for sb in reversed(range(len(groups))):
        log_beta, ct, mask = groups[sb]
        suffix, total = ct[:, :LANES], ct[:, LANES:]
        a = jnp.exp2(log_beta + (suffix if r is None else r + suffix))
        if mask is not None:
            a = jnp.where(mask, a, 0.0)
        parts[sb] = a.astype(BF16)
        r = total if r is None else r + total
    return r, jnp.concatenate(parts, axis=1)


def _sb_tile(head, uj, row0, nrows, k0, nkeys):
    q_ref, k_ref, v_ref, r_ref, acc_ref = head
    rows = slice(row0, row0 + nrows)
    groups = _sb_suffix_sums(_sb_logits(q_ref, k_ref, row0, nrows, k0, nkeys), uj, False)
    r, a = _sb_weights(r_ref[rows, :], groups)
    r_ref[rows, :] = r
    acc_ref[rows, :] += _dot(a, v_ref[pl.ds(k0, nkeys), :])


def _sb_band(heads, uj, d0, has_history):
    tiles = []
    for head in heads:
        for rt in range(TQ // TD):
            kts = [rt] + ([rt - 1] if rt > 0 or has_history else [])
            tiles.append((head, rt * TD, [pl.multiple_of(d0 + kt * TD, TD) for kt in kts]))
    logits = [[_sb_logits(head[0], head[1], row0, TD, k0, TD) for k0 in k0s]
              for head, row0, k0s in tiles]
    groups = [[_sb_suffix_sums(w, uj, n == 0) for n, w in enumerate(ws)] for ws in logits]
    for (head, row0, k0s), tile_groups in zip(tiles, groups):
        _, _, v_ref, r_ref, acc_ref = head
        r, acc = None, None
        for k0, g in zip(k0s, tile_groups):
            r, a = _sb_weights(r, g)
            pv = _dot(a, v_ref[pl.ds(k0, TD), :])
            acc = pv if acc is None else acc + pv
        r_ref[row0:row0 + TD, :] = r
        acc_ref[row0:row0 + TD, :] = acc


def _sb_kernel(q_ref, k_ref, v_ref, uj_ref, o_ref, acc_ref, r_ref):
    i = pl.program_id(2)
    uj = uj_ref[...]
    dh = acc_ref.shape[2]
    nt = TQ // TD
    d0 = pl.multiple_of(i * TQ, TQ)
    heads = []
    for hd in range(HEADS_PER_STEP):
        cols = slice(hd * dh, (hd + 1) * dh)
        heads.append((q_ref.at[:, cols], k_ref.at[:, cols], v_ref.at[:, cols],
                      r_ref.at[hd], acc_ref.at[hd]))

    @pl.when(i == 0)
    def _():
        _sb_band(heads, uj, d0, False)

    @pl.when(i > 0)
    def _():
        _sb_band(heads, uj, d0, True)

    for hd, head in enumerate(heads):
        def live():
            return jnp.max(r_ref[hd]) > DEAD_STICK_LOG2

        @pl.when(jnp.logical_and(i > 0, live()))
        def _():
            for rt in range(1, nt):
                for kt in reversed(range(-1, rt - 1)):
                    _sb_tile(head, uj, rt * TD, TD, pl.multiple_of(d0 + kt * TD, TD), TD)

            def cond(carry):
                kb, alive = carry
                return jnp.logical_and(kb >= 0, alive)

            def body(carry):
                kb, _ = carry
                _sb_tile(head, uj, 0, TQ, pl.multiple_of(kb * TD, TD), TD)
                return kb - 1, live()

            lax.while_loop(cond, body, (i * nt - 2, live()))

        o_ref[:, hd * dh:(hd + 1) * dh] = acc_ref[hd].astype(BF16)


def _suffix_sum_matrix():
    j = lax.broadcasted_iota(jnp.int32, (2 * LANES, 2 * LANES), 0) % LANES
    s = lax.broadcasted_iota(jnp.int32, (2 * LANES, 2 * LANES), 1)
    return jnp.where((s >= LANES) | (j > s), 1.0, 0.0).astype(BF16)


def _sb_attention(q, k, v, batch, seq):
    m, d = q.shape
    dh = d // N_HEADS
    dw = HEADS_PER_STEP * dh
    q3, k3, v3 = (t.reshape(batch, seq, d) for t in (q, k, v))
    kv_spec = pl.BlockSpec((None, seq, dw), lambda b, h, i: (b, 0, h))
    qo_spec = pl.BlockSpec((None, TQ, dw), lambda b, h, i: (b, i, h))
    o = pl.pallas_call(
        _sb_kernel,
        grid=(batch, N_HEADS // HEADS_PER_STEP, seq // TQ),
        in_specs=[qo_spec, kv_spec, kv_spec,
                  pl.BlockSpec((2 * LANES, 2 * LANES), lambda b, h, i: (0, 0))],
        out_specs=qo_spec,
        out_shape=jax.ShapeDtypeStruct((batch, seq, d), BF16),
        scratch_shapes=[pltpu.VMEM((HEADS_PER_STEP, TQ, dh), F32),
                        pltpu.VMEM((HEADS_PER_STEP, TQ, LANES), F32)],
        compiler_params=_params("arbitrary", "arbitrary", "arbitrary"),
        name="sb_attention",
    )(q3, k3, v3, _suffix_sum_matrix())
    return o.reshape(m, d)


def _proj_res_kernel(a_ref, w_ref, x_ref, o_ref):
    o_ref[...] = x_ref[...] + _dot(a_ref[...], w_ref[...])


def _proj_residual(a, w, layer, x):
    m, d = x.shape
    return pl.pallas_call(
        _proj_res_kernel,
        grid=(m // TM_RES,),
        in_specs=[
            pl.BlockSpec((TM_RES, d), lambda i: (i, 0)),
            pl.BlockSpec((None, d, d), lambda i: (layer, 0, 0)),
            pl.BlockSpec((TM_RES, d), lambda i: (i, 0)),
        ],
        out_specs=pl.BlockSpec((TM_RES, d), lambda i: (i, 0)),
        out_shape=jax.ShapeDtypeStruct((m, d), F32),
        compiler_params=_params("arbitrary"),
        name="proj_residual",
    )(a, w, x)


def _pool_kernel(x_ref, g_ref, w_ref, sc_ref, o_ref, buf_ref):
    j = pl.program_id(1)
    ts, d = x_ref.shape
    gw = d // len(POOL_WINDOWS)

    @pl.when(j == 0)
    def _():
        buf_ref[0:HALO, :] = jnp.zeros((HALO, d), F32)

    @pl.when(j > 0)
    def _():
        buf_ref[0:HALO, :] = buf_ref[ts:ts + HALO, :]

    x = x_ref[...]
    buf_ref[HALO:HALO + ts, :] = _rmsnorm(x, g_ref[...])
    pos1 = j * ts + 1 + lax.broadcasted_iota(jnp.int32, (ts, 1), 0)
    for gi, win in enumerate(POOL_WINDOWS):
        cols = slice(gi * gw, (gi + 1) * gw)
        ext = buf_ref[:, cols]
        tot = ext
        lag = 1
        while lag < win:
            tot = tot + pltpu.roll(tot, lag, 0)
            lag *= 2
        tot = tot[HALO:, :]
        cur = ext[HALO:, :]
        cnt = jnp.minimum(pos1, win).astype(F32)
        piece = (tot / cnt - cur).astype(BF16)
        y = _dot(piece, w_ref[gi])
        o_ref[:, cols] = x[:, cols] + y * sc_ref[:, cols]


def _pool_mixer(x, g, w, layer, sc, batch, seq):
    m, d = x.shape
    ng = len(POOL_WINDOWS)
    gw = d // ng
    nt = seq // TS_POOL
    xspec = pl.BlockSpec((TS_POOL, d), lambda b, j: (b * nt + j, 0))
    vspec = pl.BlockSpec((1, d), lambda b, j: (0, 0))
    return pl.pallas_call(
        _pool_kernel,
        grid=(batch, nt),
        in_specs=[xspec, vspec,
                  pl.BlockSpec((None, ng, gw, gw), lambda b, j: (layer, 0, 0, 0)),
                  vspec],
        out_specs=xspec,
        out_shape=jax.ShapeDtypeStruct((m, d), F32),
        scratch_shapes=[pltpu.VMEM((HALO + TS_POOL, d), F32)],
        compiler_params=_params("arbitrary", "arbitrary"),
        name="pool_mixer",
    )(x, g, w, sc)


def _conv_in_kernel(tiles_per_seq, x_ref, g_ref, wb_ref, wc_ref, wu_ref, cw_ref,
                    o_ref, h_ref, carry_ref):
    i = pl.program_id(0)
    j = pl.program_id(1)

    @pl.when(i % tiles_per_seq == 0)
    def _():
        carry_ref[j] = jnp.zeros(carry_ref.shape[1:], F32)

    def gated_conv(h):
        cu = _dot(h, wc_ref[...]) * _dot(h, wu_ref[...])
        tm = cu.shape[0]
        prev = carry_ref[j]
        p1 = prev[SUBLANES - 1:SUBLANES, :]
        p2 = prev[SUBLANES - 2:SUBLANES - 1, :]
        rows = lax.broadcasted_iota(jnp.int32, cu.shape, 0)
        lag1 = jnp.where(rows == 0, p1, pltpu.roll(cu, 1, 0))
        lag2 = jnp.where(rows == 0, p2, jnp.where(rows == 1, p1, pltpu.roll(cu, 2, 0)))
        cw = cw_ref[...]
        y = cw[0:1, :] * lag2 + cw[1:2, :] * lag1 + cw[2:3, :] * cu
        carry_ref[j] = cu[tm - SUBLANES:tm, :]
        o_ref[...] = (_dot(h, wb_ref[...]) * y).astype(BF16)

    @pl.when(j == 0)
    def _():
        h = _rmsnorm(x_ref[...], g_ref[...]).astype(BF16)
        h_ref[...] = h
        gated_conv(h)

    @pl.when(j > 0)
    def _():
        gated_conv(h_ref[...])


def _conv_in(x, g, w3, layer, cw, seq):
    m, d = x.shape
    nj = d // TN_PROJ

    def wspec(part):
        return pl.BlockSpec((None, d, TN_PROJ),
                            lambda i, j: (layer, 0, part * nj + j))

    return pl.pallas_call(
        functools.partial(_conv_in_kernel, seq // TM_PROJ),
        grid=(m // TM_PROJ, nj),
        in_specs=[
            pl.BlockSpec((TM_PROJ, d), lambda i, j: (i, 0)),
            pl.BlockSpec((1, d), lambda i, j: (0, 0)),
            wspec(0), wspec(1), wspec(2),
            pl.BlockSpec((None, CONV_W, TN_PROJ), lambda i, j: (layer, 0, j)),
        ],
        out_specs=pl.BlockSpec((TM_PROJ, TN_PROJ), lambda i, j: (i, j)),
        out_shape=jax.ShapeDtypeStruct((m, d), BF16),
        scratch_shapes=[pltpu.VMEM((TM_PROJ, d), BF16),
                        pltpu.VMEM((nj, SUBLANES, TN_PROJ), F32)],
        compiler_params=_params("arbitrary", "arbitrary"),
        name="conv_in",
    )(x, g, w3, w3, w3, cw)


def _ffn_step(x_ref, g_ref, weights, o_ref, h_ref):
    def hidden_tile(h):
        wg, wu, wd = weights()
        gate = _dot(h, wg)
        up = _dot(h, wu)
        act = (gate * jax.nn.sigmoid(gate) * up).astype(BF16)
        return _dot(act, wd)

    @pl.when(pl.program_id(1) == 0)
    def _():
        x = x_ref[...]
        h = _rmsnorm(x, g_ref[...]).astype(BF16)
        h_ref[...] = h
        o_ref[...] = x + hidden_tile(h)

    @pl.when(pl.program_id(1) > 0)
    def _():
        o_ref[...] += hidden_tile(h_ref[...])


def _ffn_head_kernel(x_ref, g_ref, wg_ref, wu_ref, wd_ref,
                     o_ref, wg_out, wu_out, wd_out, h_ref):
    def weights():
        wg, wu, wd = (w[...].astype(BF16) for w in (wg_ref, wu_ref, wd_ref))
        wg_out[...], wu_out[...], wd_out[...] = wg, wu, wd
        return wg, wu, wd

    _ffn_step(x_ref, g_ref, weights, o_ref, h_ref)


def _ffn_kernel(x_ref, g_ref, wg_ref, wu_ref, wd_ref, o_ref, h_ref):
    _ffn_step(x_ref, g_ref, lambda: (wg_ref[...], wu_ref[...], wd_ref[...]), o_ref, h_ref)


def _ffn(x, g, wg32, wu32, wd32, layer):
    m, d = x.shape
    f = wg32.shape[2]
    once = pl.Buffered(1)
    y, wg, wu, wd = pl.pallas_call(
        _ffn_head_kernel,
        grid=(1, f // TF_HEAD),
        in_specs=[
            pl.BlockSpec((TM_FFN, d), lambda i, j: (0, 0), pipeline_mode=once),
            pl.BlockSpec((1, d), lambda i, j: (0, 0)),
            pl.BlockSpec((None, d, TF_HEAD), lambda i, j: (layer, 0, j)),
            pl.BlockSpec((None, d, TF_HEAD), lambda i, j: (layer, 0, j)),
            pl.BlockSpec((None, TF_HEAD, d), lambda i, j: (layer, j, 0)),
        ],
        out_specs=[
            pl.BlockSpec((TM_FFN, d), lambda i, j: (0, 0)),
            pl.BlockSpec((d, TF_HEAD), lambda i, j: (0, j)),
            pl.BlockSpec((d, TF_HEAD), lambda i, j: (0, j)),
            pl.BlockSpec((TF_HEAD, d), lambda i, j: (j, 0)),
        ],
        out_shape=[jax.ShapeDtypeStruct((m, d), F32),
                   jax.ShapeDtypeStruct((d, f), BF16),
                   jax.ShapeDtypeStruct((d, f), BF16),
                   jax.ShapeDtypeStruct((f, d), BF16)],
        input_output_aliases={0: 0},
        scratch_shapes=[pltpu.VMEM((TM_FFN, d), BF16)],
        compiler_params=_params("arbitrary", "arbitrary"),
        name="ffn_head",
    )(x, g, wg32, wu32, wd32)
    return pl.pallas_call(
        _ffn_kernel,
        grid=(m // TM_FFN - 1, f // TF_FFN),
        in_specs=[
            pl.BlockSpec((TM_FFN, d), lambda i, j: (i + 1, 0)),
            pl.BlockSpec((1, d), lambda i, j: (0, 0)),
            pl.BlockSpec((d, TF_FFN), lambda i, j: (0, j)),
            pl.BlockSpec((d, TF_FFN), lambda i, j: (0, j)),
            pl.BlockSpec((TF_FFN, d), lambda i, j: (j, 0)),
        ],
        out_specs=pl.BlockSpec((TM_FFN, d), lambda i, j: (i + 1, 0)),
        out_shape=jax.ShapeDtypeStruct((m, d), F32),
        input_output_aliases={0: 0},
        scratch_shapes=[pltpu.VMEM((TM_FFN, d), BF16)],
        compiler_params=_params("arbitrary", "arbitrary"),
        name="ffn",
    )(y, g, wg, wu, wd)


def kernel(x, norm_mix_g, norm_ffn_g, sb_w_qkv, sb_g_q, sb_g_k, sb_w_o, pool_w, pool_scale, conv_w_in, conv_w, conv_w_out, ffn_w_gate, ffn_w_up, ffn_w_down):
    batch, seq, d = x.shape
    depth = norm_mix_g.shape[0]
    xf = x.reshape(batch * seq, d)
    sb_w_qkv, sb_w_o, pool_w, conv_w_in, conv_w_out = (
        w.astype(BF16) for w in (sb_w_qkv, sb_w_o, pool_w, conv_w_in, conv_w_out))
    for i in range(depth):
        kind, j = i % N_MIXERS, i // N_MIXERS
        g_mix = norm_mix_g[i:i + 1]
        if kind == 0:
            q, k, v = _qkv_proj(xf, g_mix, sb_w_qkv, j, sb_g_q[j:j + 1], sb_g_k[j:j + 1])
            o = _sb_attention(q, k, v, batch, seq)
            xf = _proj_residual(o, sb_w_o, j, xf)
        elif kind == 1:
            xf = _pool_mixer(xf, g_mix, pool_w, j, pool_scale[j:j + 1], batch, seq)
        else:
            gated = _conv_in(xf, g_mix, conv_w_in, j, conv_w, seq)
            xf = _proj_residual(gated, conv_w_out, j, xf)
        xf = _ffn(xf, norm_ffn_g[i:i + 1], ffn_w_gate, ffn_w_up, ffn_w_down, i)
    return xf.reshape(batch, seq, d)
```

```python
import functools

import jax
import jax.numpy as jnp
from jax import lax
from jax.experimental import pallas as pl
from jax.experimental.pallas import tpu as pltpu

F32 = jnp.float32
BF16 = jnp.bfloat16

EPS = 1e-6
LOG2E = 1.4426950408889634
DEAD_STICK_LOG2 = -160.0
N_HEADS = 16
POOL_WINDOWS = (2, 4, 8, 16)
CONV_W = 3
N_MIXERS = 3

LANES = 128
SUBLANES = 8
VMEM_LIMIT_BYTES = 56 * 1024 * 1024

TM_PROJ = 1024
TN_PROJ = 512
TM_RES = 512
TM_FFN = 1024
TF_FFN = 512
TF_HEAD = 256
TS_POOL = 512
TQ = 1024
TD = 256
HEADS_PER_STEP = 4
HALO = 16
assert all(w & (w - 1) == 0 and w <= HALO for w in POOL_WINDOWS)


def _params(*sem):
    return pltpu.CompilerParams(dimension_semantics=sem,
                                vmem_limit_bytes=VMEM_LIMIT_BYTES)


def _rmsnorm(x, g):
    ms = jnp.mean(x * x, axis=-1, keepdims=True)
    return (x * lax.rsqrt(ms + EPS)) * g


def _dot(a, b):
    return jnp.dot(a, b, preferred_element_type=F32)


def _qkv_kernel(x_ref, g_ref, wq_ref, wk_ref, wv_ref, gq_ref, gk_ref,
                q_ref, k_ref, v_ref, h_ref):
    def head_norm(y, gh, out_ref):
        for c in range(y.shape[1] // LANES):
            yc = y[:, c * LANES:(c + 1) * LANES]
            ms = jnp.mean(yc * yc, axis=-1, keepdims=True)
            out_ref[:, c * LANES:(c + 1) * LANES] = (
                yc * lax.rsqrt(ms + EPS) * gh).astype(BF16)

    def project(h):
        head_norm(_dot(h, wq_ref[...]), gq_ref[...], q_ref)
        head_norm(_dot(h, wk_ref[...]), gk_ref[...], k_ref)
        v_ref[...] = _dot(h, wv_ref[...]).astype(BF16)

    @pl.when(pl.program_id(1) == 0)
    def _():
        h = _rmsnorm(x_ref[...], g_ref[...]).astype(BF16)
        h_ref[...] = h
        project(h)

    @pl.when(pl.program_id(1) > 0)
    def _():
        project(h_ref[...])


def _qkv_proj(x, g, w3, layer, gq, gk):
    m, d = x.shape
    nj = d // TN_PROJ
    out = jax.ShapeDtypeStruct((m, d), BF16)

    def wspec(part):
        return pl.BlockSpec((None, d, TN_PROJ),
                            lambda i, j: (layer, 0, part * nj + j))

    ospec = pl.BlockSpec((TM_PROJ, TN_PROJ), lambda i, j: (i, j))
    return pl.pallas_call(
        _qkv_kernel,
        grid=(m // TM_PROJ, nj),
        in_specs=[
            pl.BlockSpec((TM_PROJ, d), lambda i, j: (i, 0)),
            pl.BlockSpec((1, d), lambda i, j: (0, 0)),
            wspec(0), wspec(1), wspec(2),
            pl.BlockSpec((1, LANES), lambda i, j: (0, 0)),
            pl.BlockSpec((1, LANES), lambda i, j: (0, 0)),
        ],
        out_specs=[ospec, ospec, ospec],
        out_shape=[out, out, out],
        scratch_shapes=[pltpu.VMEM((TM_PROJ, d), BF16)],
        compiler_params=_params("arbitrary", "arbitrary"),
        name="qkv_proj",
    )(x, g, w3, w3, w3, gq, gk)


def _sb_logits(q_ref, k_ref, row0, nrows, k0, nkeys):
    q = q_ref[row0:row0 + nrows, :]
    kblk = k_ref[pl.ds(k0, nkeys), :]
    return lax.dot_general(q, kblk, (((1,), (1,)), ((), ())),
                           preferred_element_type=F32) * (q.shape[1] ** -0.5 * LOG2E)


def _sb_suffix_sums(w, uj, masked):
    groups = []
    for sb in range(w.shape[1] // LANES):
        ws = w[:, sb * LANES:(sb + 1) * LANES]
        lse = jnp.log2(1.0 + jnp.exp2(-jnp.abs(ws)))
        log_beta = jnp.minimum(ws, 0.0) - lse
        log_keep = log_beta - ws
        mask = None
        if masked:
            t_idx = lax.broadcasted_iota(jnp.int32, ws.shape, 0)
            s_idx = lax.broadcasted_iota(jnp.int32, ws.shape, 1) + sb * LANES
            mask = s_idx < t_idx
            log_keep = jnp.where(mask, log_keep, 0.0)
        hi = log_keep.astype(BF16)
        lo = (log_keep - hi.astype(F32)).astype(BF16)
        ct = _dot(jnp.concatenate([hi, lo], axis=1), uj)
        groups.append((log_beta, ct, mask))
    return groups


def _sb_weights(r, groups):
    parts = [None] * len(groups)
    for sb in reversed(range(len(groups))):
        log_beta, ct, mask = groups[sb]
        suffix, total = ct[:, :LANES], ct[:, LANES:]
        a = jnp.exp2(log_beta + (suffix if r is None else r + suffix))
        if mask is not None:
            a = jnp.where(mask, a, 0.0)
        parts[sb] = a.astype(BF16)
        r = total if r is None else r + total
    return r, jnp.concatenate(parts, axis=1)


def _sb_tile(head, uj, row0, nrows, k0, nkeys):
    q_ref, k_ref, v_ref, r_ref, acc_ref = head
    rows = slice(row0, row0 + nrows)
    groups = _sb_suffix_sums(_sb_logits(q_ref, k_ref, row0, nrows, k0, nkeys), uj, False)
    r, a = _sb_weights(r_ref[rows, :], groups)
    r_ref[rows, :] = r
    acc_ref[rows, :] += _dot(a, v_ref[pl.ds(k0, nkeys), :])


def _sb_band(heads, uj, d0, has_history):
    tiles = []
    for head in heads:
        for rt in range(TQ // TD):
            kts = [rt] + ([rt - 1] if rt > 0 or has_history else [])
            tiles.append((head, rt * TD, [pl.multiple_of(d0 + kt * TD, TD) for kt in kts]))
    logits = [[_sb_logits(head[0], head[1], row0, TD, k0, TD) for k0 in k0s]
              for head, row0, k0s in tiles]
    groups = [[_sb_suffix_sums(w, uj, n == 0) for n, w in enumerate(ws)] for ws in logits]
    for (head, row0, k0s), tile_groups in zip(tiles, groups):
        _, _, v_ref, r_ref, acc_ref = head
        r, acc = None, None
        for k0, g in zip(k0s, tile_groups):
            r, a = _sb_weights(r, g)
            pv = _dot(a, v_ref[pl.ds(k0, TD), :])
            acc = pv if acc is None else acc + pv
        r_ref[row0:row0 + TD, :] = r
        acc_ref[row0:row0 + TD, :] = acc


def _sb_kernel(q_ref, k_ref, v_ref, uj_ref, o_ref, acc_ref, r_ref):
    i = pl.program_id(2)
    uj = uj_ref[...]
    dh = acc_ref.shape[2]
    nt = TQ // TD
    d0 = pl.multiple_of(i * TQ, TQ)
    heads = []
    for hd in range(HEADS_PER_STEP):
        cols = slice(hd * dh, (hd + 1) * dh)
        heads.append((q_ref.at[:, cols], k_ref.at[:, cols], v_ref.at[:, cols],
                      r_ref.at[hd], acc_ref.at[hd]))

    @pl.when(i == 0)
    def _():
        _sb_band(heads, uj, d0, False)

    @pl.when(i > 0)
    def _():
        _sb_band(heads, uj, d0, True)

    for hd, head in enumerate(heads):
        def live():
            return jnp.max(r_ref[hd]) > DEAD_STICK_LOG2

        @pl.when(jnp.logical_and(i > 0, live()))
        def _():
            for rt in range(1, nt):
                for kt in reversed(range(-1, rt - 1)):
                    _sb_tile(head, uj, rt * TD, TD, pl.multiple_of(d0 + kt * TD, TD), TD)

            def cond(carry):
                kb, alive = carry
                return jnp.logical_and(kb >= 0, alive)

            def body(carry):
                kb, _ = carry
                _sb_tile(head, uj, 0, TQ, pl.multiple_of(kb * TD, TD), TD)
                return kb - 1, live()

            lax.while_loop(cond, body, (i * nt - 2, live()))

        o_ref[:, hd * dh:(hd + 1) * dh] = acc_ref[hd].astype(BF16)


def _suffix_sum_matrix():
    j = lax.broadcasted_iota(jnp.int32, (2 * LANES, 2 * LANES), 0) % LANES
    s = lax.broadcasted_iota(jnp.int32, (2 * LANES, 2 * LANES), 1)
    return jnp.where((s >= LANES) | (j > s), 1.0, 0.0).astype(BF16)


def _sb_attention(q, k, v, batch, seq):
    m, d = q.shape
    dh = d // N_HEADS
    dw = HEADS_PER_STEP * dh
    q3, k3, v3 = (t.reshape(batch, seq, d) for t in (q, k, v))
    kv_spec = pl.BlockSpec((None, seq, dw), lambda b, h, i: (b, 0, h))
    qo_spec = pl.BlockSpec((None, TQ, dw), lambda b, h, i: (b, i, h))
    o = pl.pallas_call(
        _sb_kernel,
        grid=(batch, N_HEADS // HEADS_PER_STEP, seq // TQ),
        in_specs=[qo_spec, kv_spec, kv_spec,
                  pl.BlockSpec((2 * LANES, 2 * LANES), lambda b, h, i: (0, 0))],
        out_specs=qo_spec,
        out_shape=jax.ShapeDtypeStruct((batch, seq, d), BF16),
        scratch_shapes=[pltpu.VMEM((HEADS_PER_STEP, TQ, dh), F32),
                        pltpu.VMEM((HEADS_PER_STEP, TQ, LANES), F32)],
        compiler_params=_params("arbitrary", "arbitrary", "arbitrary"),
        name="sb_attention",
    )(q3, k3, v3, _suffix_sum_matrix())
    return o.reshape(m, d)


def _proj_res_kernel(a_ref, w_ref, x_ref, o_ref):
    o_ref[...] = x_ref[...] + _dot(a_ref[...], w_ref[...])


def _proj_residual(a, w, layer, x):
    m, d = x.shape
    return pl.pallas_call(
        _proj_res_kernel,
        grid=(m // TM_RES,),
        in_specs=[
            pl.BlockSpec((TM_RES, d), lambda i: (i, 0)),
            pl.BlockSpec((None, d, d), lambda i: (layer, 0, 0)),
            pl.BlockSpec((TM_RES, d), lambda i: (i, 0)),
        ],
        out_specs=pl.BlockSpec((TM_RES, d), lambda i: (i, 0)),
        out_shape=jax.ShapeDtypeStruct((m, d), F32),
        compiler_params=_params("arbitrary"),
        name="proj_residual",
    )(a, w, x)


def _pool_kernel(x_ref, g_ref, w_ref, sc_ref, o_ref, buf_ref):
    j = pl.program_id(1)
    ts, d = x_ref.shape
    gw = d // len(POOL_WINDOWS)

    @pl.when(j == 0)
    def _():
        buf_ref[0:HALO, :] = jnp.zeros((HALO, d), F32)

    @pl.when(j > 0)
    def _():
        buf_ref[0:HALO, :] = buf_ref[ts:ts + HALO, :]

    x = x_ref[...]
    buf_ref[HALO:HALO + ts, :] = _rmsnorm(x, g_ref[...])
    pos1 = j * ts + 1 + lax.broadcasted_iota(jnp.int32, (ts, 1), 0)
    for gi, win in enumerate(POOL_WINDOWS):
        cols = slice(gi * gw, (gi + 1) * gw)
        ext = buf_ref[:, cols]
        tot = ext
        lag = 1
        while lag < win:
            tot = tot + pltpu.roll(tot, lag, 0)
            lag *= 2
        tot = tot[HALO:, :]
        cur = ext[HALO:, :]
        cnt = jnp.minimum(pos1, win).astype(F32)
        piece = (tot / cnt - cur).astype(BF16)
        y = _dot(piece, w_ref[gi])
        o_ref[:, cols] = x[:, cols] + y * sc_ref[:, cols]


def _pool_mixer(x, g, w, layer, sc, batch, seq):
    m, d = x.shape
    ng = len(POOL_WINDOWS)
    gw = d // ng
    nt = seq // TS_POOL
    xspec = pl.BlockSpec((TS_POOL, d), lambda b, j: (b * nt + j, 0))
    vspec = pl.BlockSpec((1, d), lambda b, j: (0, 0))
    return pl.pallas_call(
        _pool_kernel,
        grid=(batch, nt),
        in_specs=[xspec, vspec,
                  pl.BlockSpec((None, ng, gw, gw), lambda b, j: (layer, 0, 0, 0)),
                  vspec],
        out_specs=xspec,
        out_shape=jax.ShapeDtypeStruct((m, d), F32),
        scratch_shapes=[pltpu.VMEM((HALO + TS_POOL, d), F32)],
        compiler_params=_params("arbitrary", "arbitrary"),
        name="pool_mixer",
    )(x, g, w, sc)


def _conv_in_kernel(tiles_per_seq, x_ref, g_ref, wb_ref, wc_ref, wu_ref, cw_ref,
                    o_ref, h_ref, carry_ref):
    i = pl.program_id(0)
    j = pl.program_id(1)

    @pl.when(i % tiles_per_seq == 0)
    def _():
        carry_ref[j] = jnp.zeros(carry_ref.shape[1:], F32)

    def gated_conv(h):
        cu = _dot(h, wc_ref[...]) * _dot(h, wu_ref[...])
        tm = cu.shape[0]
        prev = carry_ref[j]
        p1 = prev[SUBLANES - 1:SUBLANES, :]
        p2 = prev[SUBLANES - 2:SUBLANES - 1, :]
        rows = lax.broadcasted_iota(jnp.int32, cu.shape, 0)
        lag1 = jnp.where(rows == 0, p1, pltpu.roll(cu, 1, 0))
        lag2 = jnp.where(rows == 0, p2, jnp.where(rows == 1, p1, pltpu.roll(cu, 2, 0)))
        cw = cw_ref[...]
        y = cw[0:1, :] * lag2 + cw[1:2, :] * lag1 + cw[2:3, :] * cu
        carry_ref[j] = cu[tm - SUBLANES:tm, :]
        o_ref[...] = (_dot(h, wb_ref[...]) * y).astype(BF16)

    @pl.when(j == 0)
    def _():
        h = _rmsnorm(x_ref[...], g_ref[...]).astype(BF16)
        h_ref[...] = h
        gated_conv(h)

    @pl.when(j > 0)
    def _():
        gated_conv(h_ref[...])


def _conv_in(x, g, w3, layer, cw, seq):
    m, d = x.shape
    nj = d // TN_PROJ

    def wspec(part):
        return pl.BlockSpec((None, d, TN_PROJ),
                            lambda i, j: (layer, 0, part * nj + j))

    return pl.pallas_call(
        functools.partial(_conv_in_kernel, seq // TM_PROJ),
        grid=(m // TM_PROJ, nj),
        in_specs=[
            pl.BlockSpec((TM_PROJ, d), lambda i, j: (i, 0)),
            pl.BlockSpec((1, d), lambda i, j: (0, 0)),
            wspec(0), wspec(1), wspec(2),
            pl.BlockSpec((None, CONV_W, TN_PROJ), lambda i, j: (layer, 0, j)),
        ],
        out_specs=pl.BlockSpec((TM_PROJ, TN_PROJ), lambda i, j: (i, j)),
        out_shape=jax.ShapeDtypeStruct((m, d), BF16),
        scratch_shapes=[pltpu.VMEM((TM_PROJ, d), BF16),
                        pltpu.VMEM((nj, SUBLANES, TN_PROJ), F32)],
        compiler_params=_params("arbitrary", "arbitrary"),
        name="conv_in",
    )(x, g, w3, w3, w3, cw)


def _ffn_step(x_ref, g_ref, weights, o_ref, h_ref):
    def hidden_tile(h):
        wg, wu, wd = weights()
        gate = _dot(h, wg)
        up = _dot(h, wu)
        act = (gate * jax.nn.sigmoid(gate) * up).astype(BF16)
        return _dot(act, wd)

    @pl.when(pl.program_id(1) == 0)
    def _():
        x = x_ref[...]
        h = _rmsnorm(x, g_ref[...]).astype(BF16)
        h_ref[...] = h
        o_ref[...] = x + hidden_tile(h)

    @pl.when(pl.program_id(1) > 0)
    def _():
        o_ref[...] += hidden_tile(h_ref[...])


def _ffn_head_kernel(x_ref, g_ref, wg_ref, wu_ref, wd_ref,
                     o_ref, wg_out, wu_out, wd_out, h_ref):
    def weights():
        wg, wu, wd = (w[...].astype(BF16) for w in (wg_ref, wu_ref, wd_ref))
        wg_out[...], wu_out[...], wd_out[...] = wg, wu, wd
        return wg, wu, wd

    _ffn_step(x_ref, g_ref, weights, o_ref, h_ref)


def _ffn_kernel(x_ref, g_ref, wg_ref, wu_ref, wd_ref, o_ref, h_ref):
    _ffn_step(x_ref, g_ref, lambda: (wg_ref[...], wu_ref[...], wd_ref[...]), o_ref, h_ref)


def _ffn(x, g, wg32, wu32, wd32, layer):
    m, d = x.shape
    f = wg32.shape[2]
    once = pl.Buffered(1)
    y, wg, wu, wd = pl.pallas_call(
        _ffn_head_kernel,
        grid=(1, f // TF_HEAD),
        in_specs=[
            pl.BlockSpec((TM_FFN, d), lambda i, j: (0, 0), pipeline_mode=once),
            pl.BlockSpec((1, d), lambda i, j: (0, 0)),
            pl.BlockSpec((None, d, TF_HEAD), lambda i, j: (layer, 0, j)),
            pl.BlockSpec((None, d, TF_HEAD), lambda i, j: (layer, 0, j)),
            pl.BlockSpec((None, TF_HEAD, d), lambda i, j: (layer, j, 0)),
        ],
        out_specs=[
            pl.BlockSpec((TM_FFN, d), lambda i, j: (0, 0)),
            pl.BlockSpec((d, TF_HEAD), lambda i, j: (0, j)),
            pl.BlockSpec((d, TF_HEAD), lambda i, j: (0, j)),
            pl.BlockSpec((TF_HEAD, d), lambda i, j: (j, 0)),
        ],
        out_shape=[jax.ShapeDtypeStruct((m, d), F32),
                   jax.ShapeDtypeStruct((d, f), BF16),
                   jax.ShapeDtypeStruct((d, f), BF16),
                   jax.ShapeDtypeStruct((f, d), BF16)],
        input_output_aliases={0: 0},
        scratch_shapes=[pltpu.VMEM((TM_FFN, d), BF16)],
        compiler_params=_params("arbitrary", "arbitrary"),
        name="ffn_head",
    )(x, g, wg32, wu32, wd32)
    return pl.pallas_call(
        _ffn_kernel,
        grid=(m // TM_FFN - 1, f // TF_FFN),
        in_specs=[
            pl.BlockSpec((TM_FFN, d), lambda i, j: (i + 1, 0)),
            pl.BlockSpec((1, d), lambda i, j: (0, 0)),
            pl.BlockSpec((d, TF_FFN), lambda i, j: (0, j)),
            pl.BlockSpec((d, TF_FFN), lambda i, j: (0, j)),
            pl.BlockSpec((TF_FFN, d), lambda i, j: (j, 0)),
        ],
        out_specs=pl.BlockSpec((TM_FFN, d), lambda i, j: (i + 1, 0)),
        out_shape=jax.ShapeDtypeStruct((m, d), F32),
        input_output_aliases={0: 0},
        scratch_shapes=[pltpu.VMEM((TM_FFN, d), BF16)],
        compiler_params=_params("arbitrary", "arbitrary"),
        name="ffn",
    )(y, g, wg, wu, wd)


def kernel(x, norm_mix_g, norm_ffn_g, sb_w_qkv, sb_g_q, sb_g_k, sb_w_o, pool_w, pool_scale, conv_w_in, conv_w, conv_w_out, ffn_w_gate, ffn_w_up, ffn_w_down):
    batch, seq, d = x.shape
    depth = norm_mix_g.shape[0]
    f = ffn_w_gate.shape[2]
    assert x.dtype == F32 and d // N_HEADS == LANES and N_HEADS % HEADS_PER_STEP == 0
    assert d % TN_PROJ == 0 and d % len(POOL_WINDOWS) == 0 and f % TF_FFN == 0 and f % TF_HEAD == 0
    assert all(seq % t == 0 for t in (TQ, TM_PROJ, TS_POOL)) and TQ % TD == 0
    assert all((batch * seq) % t == 0 for t in (TM_PROJ, TM_RES, TM_FFN)) and batch * seq > TM_FFN
    xf = x.reshape(batch * seq, d)
    sb_w_qkv, sb_w_o, pool_w, conv_w_in, conv_w_out = (
        w.astype(BF16) for w in (sb_w_qkv, sb_w_o, pool_w, conv_w_in, conv_w_out))
    for i in range(depth):
        kind, j = i % N_MIXERS, i // N_MIXERS
        g_mix = norm_mix_g[i:i + 1]
        if kind == 0:
            q, k, v = _qkv_proj(xf, g_mix, sb_w_qkv, j, sb_g_q[j:j + 1], sb_g_k[j:j + 1])
            o = _sb_attention(q, k, v, batch, seq)
            xf = _proj_residual(o, sb_w_o, j, xf)
        elif kind == 1:
            xf = _pool_mixer(xf, g_mix, pool_w, j, pool_scale[j:j + 1], batch, seq)
        else:
            gated = _conv_in(xf, g_mix, conv_w_in, j, conv_w, seq)
            xf = _proj_residual(gated, conv_w_out, j, xf)
        xf = _ffn(xf, norm_ffn_g[i:i + 1], ffn_w_gate, ffn_w_up, ffn_w_down, i)
    return xf.reshape(batch, seq, d)
```

```python
import functools

import jax
import jax.numpy as jnp
from jax import lax
from jax.experimental import pallas as pl
from jax.experimental.pallas import tpu as pltpu

F32 = jnp.float32
BF16 = jnp.bfloat16

EPS = 1e-6
LOG2E = 1.4426950408889634
DEAD_STICK_LOG2 = -160.0
N_HEADS = 16
POOL_WINDOWS = (2, 4, 8, 16)
CONV_W = 3
N_MIXERS = 3

LANES = 128
SUBLANES = 8
VMEM_LIMIT_BYTES = 56 * 1024 * 1024

TM_PROJ = 1024
TN_PROJ = 512
TM_RES = 512
TM_FFN = 1024
TF_FFN = 512
TF_HEAD = 256
HEAD_SLOTS = 3
TS_POOL = 512
TQ = 1024
TD = 256
HEADS_PER_STEP = 4
HALO = 16
assert all(w & (w - 1) == 0 and w <= HALO for w in POOL_WINDOWS)


def _params(*sem):
    return pltpu.CompilerParams(dimension_semantics=sem,
                                vmem_limit_bytes=VMEM_LIMIT_BYTES)


def _rmsnorm(x, g):
    ms = jnp.mean(x * x, axis=-1, keepdims=True)
    return (x * lax.rsqrt(ms + EPS)) * g


def _dot(a, b):
    return jnp.dot(a, b, preferred_element_type=F32)


def _qkv_kernel(x_ref, g_ref, wq_ref, wk_ref, wv_ref, gq_ref, gk_ref,
                q_ref, k_ref, v_ref, h_ref):
    def head_norm(y, gh, out_ref):
        for c in range(y.shape[1] // LANES):
            yc = y[:, c * LANES:(c + 1) * LANES]
            ms = jnp.mean(yc * yc, axis=-1, keepdims=True)
            out_ref[:, c * LANES:(c + 1) * LANES] = (
                yc * lax.rsqrt(ms + EPS) * gh).astype(BF16)

    def project(h):
        head_norm(_dot(h, wq_ref[...]), gq_ref[...], q_ref)
        head_norm(_dot(h, wk_ref[...]), gk_ref[...], k_ref)
        v_ref[...] = _dot(h, wv_ref[...]).astype(BF16)

    @pl.when(pl.program_id(1) == 0)
    def _():
        h = _rmsnorm(x_ref[...], g_ref[...]).astype(BF16)
        h_ref[...] = h
        project(h)

    @pl.when(pl.program_id(1) > 0)
    def _():
        project(h_ref[...])


def _qkv_proj(x, g, w3, layer, gq, gk):
    m, d = x.shape
    nj = d // TN_PROJ
    out = jax.ShapeDtypeStruct((m, d), BF16)

    def wspec(part):
        return pl.BlockSpec((None, d, TN_PROJ),
                            lambda i, j: (layer, 0, part * nj + j))

    ospec = pl.BlockSpec((TM_PROJ, TN_PROJ), lambda i, j: (i, j))
    return pl.pallas_call(
        _qkv_kernel,
        grid=(m // TM_PROJ, nj),
        in_specs=[
            pl.BlockSpec((TM_PROJ, d), lambda i, j: (i, 0)),
            pl.BlockSpec((1, d), lambda i, j: (0, 0)),
            wspec(0), wspec(1), wspec(2),
            pl.BlockSpec((1, LANES), lambda i, j: (0, 0)),
            pl.BlockSpec((1, LANES), lambda i, j: (0, 0)),
        ],
        out_specs=[ospec, ospec, ospec],
        out_shape=[out, out, out],
        scratch_shapes=[pltpu.VMEM((TM_PROJ, d), BF16)],
        compiler_params=_params("arbitrary", "arbitrary"),
        name="qkv_proj",
    )(x, g, w3, w3, w3, gq, gk)


def _sb_logits(q_ref, k_ref, row0, nrows, k0, nkeys):
    q = q_ref[row0:row0 + nrows, :]
    kblk = k_ref[pl.ds(k0, nkeys), :]
    return lax.dot_general(q, kblk, (((1,), (1,)), ((), ())),
                           preferred_element_type=F32) * (q.shape[1] ** -0.5 * LOG2E)


def _sb_suffix_sums(w, uj, masked):
    groups = []
    for sb in range(w.shape[1] // LANES):
        ws = w[:, sb * LANES:(sb + 1) * LANES]
        lse = jnp.log2(1.0 + jnp.exp2(-jnp.abs(ws)))
        log_beta = jnp.minimum(ws, 0.0) - lse
        log_keep = log_beta - ws
        mask = None
        if masked:
            t_idx = lax.broadcasted_iota(jnp.int32, ws.shape, 0)
            s_idx = lax.broadcasted_iota(jnp.int32, ws.shape, 1) + sb * LANES
            mask = s_idx < t_idx
            log_keep = jnp.where(mask, log_keep, 0.0)
        hi = log_keep.astype(BF16)
        lo = (log_keep - hi.astype(F32)).astype(BF16)
        ct = _dot(jnp.concatenate([hi, lo], axis=1), uj)
        groups.append((log_beta, ct, mask))
    return groups


def _sb_weights(r, groups):
    parts = [None] * len(groups)
    for sb in reversed(range(len(groups))):
        log_beta, ct, mask = groups[sb]
        suffix, total = ct[:, :LANES], ct[:, LANES:]
        a = jnp.exp2(log_beta + (suffix if r is None else r + suffix))
        if mask is not None:
            a = jnp.where(mask, a, 0.0)
        parts[sb] = a.astype(BF16)
        r = total if r is None else r + total
    return r, jnp.concatenate(parts, axis=1)


def _sb_tile(head, uj, row0, nrows, k0, nkeys):
    q_ref, k_ref, v_ref, r_ref, acc_ref = head
    rows = slice(row0, row0 + nrows)
    groups = _sb_suffix_sums(_sb_logits(q_ref, k_ref, row0, nrows, k0, nkeys), uj, False)
    r, a = _sb_weights(r_ref[rows, :], groups)
    r_ref[rows, :] = r
    acc_ref[rows, :] += _dot(a, v_ref[pl.ds(k0, nkeys), :])


def _sb_band(heads, uj, d0, has_history):
    tiles = []
    for head in heads:
        for rt in range(TQ // TD):
            kts = [rt] + ([rt - 1] if rt > 0 or has_history else [])
            tiles.append((head, rt * TD, [pl.multiple_of(d0 + kt * TD, TD) for kt in kts]))
    logits = [[_sb_logits(head[0], head[1], row0, TD, k0, TD) for k0 in k0s]
              for head, row0, k0s in tiles]
    groups = [[_sb_suffix_sums(w, uj, n == 0) for n, w in enumerate(ws)] for ws in logits]
    for (head, row0, k0s), tile_groups in zip(tiles, groups):
        _, _, v_ref, r_ref, acc_ref = head
        r, acc = None, None
        for k0, g in zip(k0s, tile_groups):
            r, a = _sb_weights(r, g)
            pv = _dot(a, v_ref[pl.ds(k0, TD), :])
            acc = pv if acc is None else acc + pv
        r_ref[row0:row0 + TD, :] = r
        acc_ref[row0:row0 + TD, :] = acc


def _sb_kernel(q_ref, k_ref, v_ref, uj_ref, o_ref, acc_ref, r_ref):
    i = pl.program_id(2)
    uj = uj_ref[...]
    dh = acc_ref.shape[2]
    nt = TQ // TD
    d0 = pl.multiple_of(i * TQ, TQ)
    heads = []
    for hd in range(HEADS_PER_STEP):
        cols = slice(hd * dh, (hd + 1) * dh)
        heads.append((q_ref.at[:, cols], k_ref.at[:, cols], v_ref.at[:, cols],
                      r_ref.at[hd], acc_ref.at[hd]))

    @pl.when(i == 0)
    def _():
        _sb_band(heads, uj, d0, False)

    @pl.when(i > 0)
    def _():
        _sb_band(heads, uj, d0, True)

    for hd, head in enumerate(heads):
        def live():
            return jnp.max(r_ref[hd]) > DEAD_STICK_LOG2

        @pl.when(jnp.logical_and(i > 0, live()))
        def _():
            for rt in range(1, nt):
                for kt in reversed(range(-1, rt - 1)):
                    _sb_tile(head, uj, rt * TD, TD, pl.multiple_of(d0 + kt * TD, TD), TD)

            def cond(carry):
                kb, alive = carry
                return jnp.logical_and(kb >= 0, alive)

            def body(carry):
                kb, _ = carry
                _sb_tile(head, uj, 0, TQ, pl.multiple_of(kb * TD, TD), TD)
                return kb - 1, live()

            lax.while_loop(cond, body, (i * nt - 2, live()))

        o_ref[:, hd * dh:(hd + 1) * dh] = acc_ref[hd].astype(BF16)


def _suffix_sum_matrix():
    j = lax.broadcasted_iota(jnp.int32, (2 * LANES, 2 * LANES), 0) % LANES
    s = lax.broadcasted_iota(jnp.int32, (2 * LANES, 2 * LANES), 1)
    return jnp.where((s >= LANES) | (j > s), 1.0, 0.0).astype(BF16)


def _sb_attention(q, k, v, batch, seq):
    m, d = q.shape
    dh = d // N_HEADS
    dw = HEADS_PER_STEP * dh
    q3, k3, v3 = (t.reshape(batch, seq, d) for t in (q, k, v))
    kv_spec = pl.BlockSpec((None, seq, dw), lambda b, h, i: (b, 0, h))
    qo_spec = pl.BlockSpec((None, TQ, dw), lambda b, h, i: (b, i, h))
    o = pl.pallas_call(
        _sb_kernel,
        grid=(batch, N_HEADS // HEADS_PER_STEP, seq // TQ),
        in_specs=[qo_spec, kv_spec, kv_spec,
                  pl.BlockSpec((2 * LANES, 2 * LANES), lambda b, h, i: (0, 0))],
        out_specs=qo_spec,
        out_shape=jax.ShapeDtypeStruct((batch, seq, d), BF16),
        scratch_shapes=[pltpu.VMEM((HEADS_PER_STEP, TQ, dh), F32),
                        pltpu.VMEM((HEADS_PER_STEP, TQ, LANES), F32)],
        compiler_params=_params("arbitrary", "arbitrary", "arbitrary"),
        name="sb_attention",
    )(q3, k3, v3, _suffix_sum_matrix())
    return o.reshape(m, d)


def _proj_res_kernel(a_ref, w_ref, x_ref, o_ref):
    o_ref[...] = x_ref[...] + _dot(a_ref[...], w_ref[...])


def _proj_residual(a, w, layer, x):
    m, d = x.shape
    return pl.pallas_call(
        _proj_res_kernel,
        grid=(m // TM_RES,),
        in_specs=[
            pl.BlockSpec((TM_RES, d), lambda i: (i, 0)),
            pl.BlockSpec((None, d, d), lambda i: (layer, 0, 0)),
            pl.BlockSpec((TM_RES, d), lambda i: (i, 0)),
        ],
        out_specs=pl.BlockSpec((TM_RES, d), lambda i: (i, 0)),
        out_shape=jax.ShapeDtypeStruct((m, d), F32),
        compiler_params=_params("arbitrary"),
        name="proj_residual",
    )(a, w, x)


def _pool_kernel(x_ref, g_ref, w_ref, sc_ref, o_ref, buf_ref):
    j = pl.program_id(1)
    ts, d = x_ref.shape
    gw = d // len(POOL_WINDOWS)

    @pl.when(j == 0)
    def _():
        buf_ref[0:HALO, :] = jnp.zeros((HALO, d), F32)

    @pl.when(j > 0)
    def _():
        buf_ref[0:HALO, :] = buf_ref[ts:ts + HALO, :]

    x = x_ref[...]
    buf_ref[HALO:HALO + ts, :] = _rmsnorm(x, g_ref[...])
    pos1 = j * ts + 1 + lax.broadcasted_iota(jnp.int32, (ts, 1), 0)
    for gi, win in enumerate(POOL_WINDOWS):
        cols = slice(gi * gw, (gi + 1) * gw)
        ext = buf_ref[:, cols]
        tot = ext
        lag = 1
        while lag < win:
            tot = tot + pltpu.roll(tot, lag, 0)
            lag *= 2
        tot = tot[HALO:, :]
        cur = ext[HALO:, :]
        cnt = jnp.minimum(pos1, win).astype(F32)
        piece = (tot / cnt - cur).astype(BF16)
        y = _dot(piece, w_ref[gi])
        o_ref[:, cols] = x[:, cols] + y * sc_ref[:, cols]


def _pool_mixer(x, g, w, layer, sc, batch, seq):
    m, d = x.shape
    ng = len(POOL_WINDOWS)
    gw = d // ng
    nt = seq // TS_POOL
    xspec = pl.BlockSpec((TS_POOL, d), lambda b, j: (b * nt + j, 0))
    vspec = pl.BlockSpec((1, d), lambda b, j: (0, 0))
    return pl.pallas_call(
        _pool_kernel,
        grid=(batch, nt),
        in_specs=[xspec, vspec,
                  pl.BlockSpec((None, ng, gw, gw), lambda b, j: (layer, 0, 0, 0)),
                  vspec],
        out_specs=xspec,
        out_shape=jax.ShapeDtypeStruct((m, d), F32),
        scratch_shapes=[pltpu.VMEM((HALO + TS_POOL, d), F32)],
        compiler_params=_params("arbitrary", "arbitrary"),
        name="pool_mixer",
    )(x, g, w, sc)


def _conv_in_kernel(tiles_per_seq, x_ref, g_ref, wb_ref, wc_ref, wu_ref, cw_ref,
                    o_ref, h_ref, carry_ref):
    i = pl.program_id(0)
    j = pl.program_id(1)

    @pl.when(i % tiles_per_seq == 0)
    def _():
        carry_ref[j] = jnp.zeros(carry_ref.shape[1:], F32)

    def gated_conv(h):
        cu = _dot(h, wc_ref[...]) * _dot(h, wu_ref[...])
        tm = cu.shape[0]
        prev = carry_ref[j]
        p1 = prev[SUBLANES - 1:SUBLANES, :]
        p2 = prev[SUBLANES - 2:SUBLANES - 1, :]
        rows = lax.broadcasted_iota(jnp.int32, cu.shape, 0)
        lag1 = jnp.where(rows == 0, p1, pltpu.roll(cu, 1, 0))
        lag2 = jnp.where(rows == 0, p2, jnp.where(rows == 1, p1, pltpu.roll(cu, 2, 0)))
        cw = cw_ref[...]
        y = cw[0:1, :] * lag2 + cw[1:2, :] * lag1 + cw[2:3, :] * cu
        carry_ref[j] = cu[tm - SUBLANES:tm, :]
        o_ref[...] = (_dot(h, wb_ref[...]) * y).astype(BF16)

    @pl.when(j == 0)
    def _():
        h = _rmsnorm(x_ref[...], g_ref[...]).astype(BF16)
        h_ref[...] = h
        gated_conv(h)

    @pl.when(j > 0)
    def _():
        gated_conv(h_ref[...])


def _conv_in(x, g, w3, layer, cw, seq):
    m, d = x.shape
    nj = d // TN_PROJ

    def wspec(part):
        return pl.BlockSpec((None, d, TN_PROJ),
                            lambda i, j: (layer, 0, part * nj + j))

    return pl.pallas_call(
        functools.partial(_conv_in_kernel, seq // TM_PROJ),
        grid=(m // TM_PROJ, nj),
        in_specs=[
            pl.BlockSpec((TM_PROJ, d), lambda i, j: (i, 0)),
            pl.BlockSpec((1, d), lambda i, j: (0, 0)),
            wspec(0), wspec(1), wspec(2),
            pl.BlockSpec((None, CONV_W, TN_PROJ), lambda i, j: (layer, 0, j)),
        ],
        out_specs=pl.BlockSpec((TM_PROJ, TN_PROJ), lambda i, j: (i, j)),
        out_shape=jax.ShapeDtypeStruct((m, d), BF16),
        scratch_shapes=[pltpu.VMEM((TM_PROJ, d), BF16),
                        pltpu.VMEM((nj, SUBLANES, TN_PROJ), F32)],
        compiler_params=_params("arbitrary", "arbitrary"),
        name="conv_in",
    )(x, g, w3, w3, w3, cw)


def _ffn_step(x_ref, g_ref, weights, o_ref, h_ref):
    def hidden_tile(h):
        wg, wu, wd = weights()
        gate = _dot(h, wg)
        up = _dot(h, wu)
        act = (gate * jax.nn.sigmoid(gate) * up).astype(BF16)
        return _dot(act, wd)

    @pl.when(pl.program_id(1) == 0)
    def _():
        x = x_ref[...]
        h = _rmsnorm(x, g_ref[...]).astype(BF16)
        h_ref[...] = h
        o_ref[...] = x + hidden_tile(h)

    @pl.when(pl.program_id(1) > 0)
    def _():
        o_ref[...] += hidden_tile(h_ref[...])


def _ffn_head_kernel(layer, x_ref, g_ref, wg_hbm, wu_hbm, wd_hbm,
                     o_ref, wg_out, wu_out, wd_out, h_ref, wg_buf, wu_buf, wd_buf, sems):
    j = pl.program_id(1)
    nsteps = pl.num_programs(1)

    def copies(tile, slot):
        cols = pl.ds(pl.multiple_of(tile * TF_HEAD, TF_HEAD), TF_HEAD)
        return (
            pltpu.make_async_copy(wg_hbm.at[layer, :, cols], wg_buf.at[slot], sems.at[0, slot]),
            pltpu.make_async_copy(wu_hbm.at[layer, :, cols], wu_buf.at[slot], sems.at[1, slot]),
            pltpu.make_async_copy(wd_hbm.at[layer, cols, :], wd_buf.at[slot], sems.at[2, slot]),
        )

    @pl.when(j == 0)
    def _():
        for ahead in range(HEAD_SLOTS - 1):
            for c in copies(ahead, ahead):
                c.start()

    nxt = j + (HEAD_SLOTS - 1)

    @pl.when(nxt < nsteps)
    def _():
        for c in copies(nxt, nxt % HEAD_SLOTS):
            c.start()

    slot = j % HEAD_SLOTS
    for c in copies(j, slot):
        c.wait()

    def weights():
        wg, wu, wd = (w[slot].astype(BF16) for w in (wg_buf, wu_buf, wd_buf))
        wg_out[...], wu_out[...], wd_out[...] = wg, wu, wd
        return wg, wu, wd

    _ffn_step(x_ref, g_ref, weights, o_ref, h_ref)


def _ffn_kernel(x_ref, g_ref, wg_ref, wu_ref, wd_ref, o_ref, h_ref):
    _ffn_step(x_ref, g_ref, lambda: (wg_ref[...], wu_ref[...], wd_ref[...]), o_ref, h_ref)


def _ffn(x, g, wg32, wu32, wd32, layer):
    m, d = x.shape
    f = wg32.shape[2]
    once = pl.Buffered(1)
    hbm = pl.BlockSpec(memory_space=pl.ANY)
    y, wg, wu, wd = pl.pallas_call(
        functools.partial(_ffn_head_kernel, layer),
        grid=(1, f // TF_HEAD),
        in_specs=[
            pl.BlockSpec((TM_FFN, d), lambda i, j: (0, 0), pipeline_mode=once),
            pl.BlockSpec((1, d), lambda i, j: (0, 0)),
            hbm, hbm, hbm,
        ],
        out_specs=[
            pl.BlockSpec((TM_FFN, d), lambda i, j: (0, 0)),
            pl.BlockSpec((d, TF_HEAD), lambda i, j: (0, j)),
            pl.BlockSpec((d, TF_HEAD), lambda i, j: (0, j)),
            pl.BlockSpec((TF_HEAD, d), lambda i, j: (j, 0)),
        ],
        out_shape=[jax.ShapeDtypeStruct((m, d), F32),
                   jax.ShapeDtypeStruct((d, f), BF16),
                   jax.ShapeDtypeStruct((d, f), BF16),
                   jax.ShapeDtypeStruct((f, d), BF16)],
        input_output_aliases={0: 0},
        scratch_shapes=[pltpu.VMEM((TM_FFN, d), BF16),
                        pltpu.VMEM((HEAD_SLOTS, d, TF_HEAD), F32),
                        pltpu.VMEM((HEAD_SLOTS, d, TF_HEAD), F32),
                        pltpu.VMEM((HEAD_SLOTS, TF_HEAD, d), F32),
                        pltpu.SemaphoreType.DMA((3, HEAD_SLOTS))],
        compiler_params=_params("arbitrary", "arbitrary"),
        name="ffn_head",
    )(x, g, wg32, wu32, wd32)
    return pl.pallas_call(
        _ffn_kernel,
        grid=(m // TM_FFN - 1, f // TF_FFN),
        in_specs=[
            pl.BlockSpec((TM_FFN, d), lambda i, j: (i + 1, 0)),
            pl.BlockSpec((1, d), lambda i, j: (0, 0)),
            pl.BlockSpec((d, TF_FFN), lambda i, j: (0, j)),
            pl.BlockSpec((d, TF_FFN), lambda i, j: (0, j)),
            pl.BlockSpec((TF_FFN, d), lambda i, j: (j, 0)),
        ],
        out_specs=pl.BlockSpec((TM_FFN, d), lambda i, j: (i + 1, 0)),
        out_shape=jax.ShapeDtypeStruct((m, d), F32),
        input_output_aliases={0: 0},
        scratch_shapes=[pltpu.VMEM((TM_FFN, d), BF16)],
        compiler_params=_params("arbitrary", "arbitrary"),
        name="ffn",
    )(y, g, wg, wu, wd)


def kernel(x, norm_mix_g, norm_ffn_g, sb_w_qkv, sb_g_q, sb_g_k, sb_w_o, pool_w, pool_scale, conv_w_in, conv_w, conv_w_out, ffn_w_gate, ffn_w_up, ffn_w_down):
    batch, seq, d = x.shape
    depth = norm_mix_g.shape[0]
    f = ffn_w_gate.shape[2]
    assert x.dtype == F32 and d // N_HEADS == LANES and N_HEADS % HEADS_PER_STEP == 0
    assert d % TN_PROJ == 0 and d % len(POOL_WINDOWS) == 0 and f % TF_FFN == 0 and f % TF_HEAD == 0
    assert all(seq % t == 0 for t in (TQ, TM_PROJ, TS_POOL)) and TQ % TD == 0
    assert all((batch * seq) % t == 0 for t in (TM_PROJ, TM_RES, TM_FFN)) and batch * seq > TM_FFN
    xf = x.reshape(batch * seq, d)
    sb_w_qkv, sb_w_o, pool_w, conv_w_in, conv_w_out = (
        w.astype(BF16) for w in (sb_w_qkv, sb_w_o, pool_w, conv_w_in, conv_w_out))
    for i in range(depth):
        kind, j = i % N_MIXERS, i // N_MIXERS
        g_mix = norm_mix_g[i:i + 1]
        if kind == 0:
            q, k, v = _qkv_proj(xf, g_mix, sb_w_qkv, j, sb_g_q[j:j + 1], sb_g_k[j:j + 1])
            o = _sb_attention(q, k, v, batch, seq)
            xf = _proj_residual(o, sb_w_o, j, xf)
        elif kind == 1:
            xf = _pool_mixer(xf, g_mix, pool_w, j, pool_scale[j:j + 1], batch, seq)
        else:
            gated = _conv_in(xf, g_mix, conv_w_in, j, conv_w, seq)
            xf = _proj_residual(gated, conv_w_out, j, xf)
        xf = _ffn(xf, norm_ffn_g[i:i + 1], ffn_w_gate, ffn_w_up, ffn_w_down, i)
    return xf.reshape(batch, seq, d)
```

```python
import functools

import jax
import jax.numpy as jnp
from jax import lax
from jax.experimental import pallas as pl
from jax.experimental.pallas import tpu as pltpu

F32 = jnp.float32
BF16 = jnp.bfloat16

EPS = 1e-6
LOG2E = 1.4426950408889634
DEAD_STICK_LOG2 = -160.0
N_HEADS = 16
POOL_WINDOWS = (2, 4, 8, 16)
CONV_W = 3
N_MIXERS = 3

LANES = 128
SUBLANES = 8
VMEM_LIMIT_BYTES = 56 * 1024 * 1024

TM_PROJ = 1024
TN_PROJ = 512
TM_RES = 512
RES_SLOTS = 3
TM_FFN = 1024
TF_FFN = 512
TF_HEAD = 256
HEAD_SLOTS = 3
TS_POOL = 512
TQ = 1024
TD = 256
HEADS_PER_STEP = 4
HALO = 16
assert all(w & (w - 1) == 0 and w <= HALO for w in POOL_WINDOWS)


def _params(*sem):
    return pltpu.CompilerParams(dimension_semantics=sem,
                                vmem_limit_bytes=VMEM_LIMIT_BYTES)


def _rmsnorm(x, g):
    ms = jnp.mean(x * x, axis=-1, keepdims=True)
    return (x * lax.rsqrt(ms + EPS)) * g


def _dot(a, b):
    return jnp.dot(a, b, preferred_element_type=F32)


def _qkv_kernel(x_ref, g_ref, wq_ref, wk_ref, wv_ref, gq_ref, gk_ref,
                q_ref, k_ref, v_ref, h_ref):
    def head_norm(y, gh, out_ref):
        for c in range(y.shape[1] // LANES):
            yc = y[:, c * LANES:(c + 1) * LANES]
            ms = jnp.mean(yc * yc, axis=-1, keepdims=True)
            out_ref[:, c * LANES:(c + 1) * LANES] = (
                yc * lax.rsqrt(ms + EPS) * gh).astype(BF16)

    def project(h):
        head_norm(_dot(h, wq_ref[...]), gq_ref[...], q_ref)
        head_norm(_dot(h, wk_ref[...]), gk_ref[...], k_ref)
        v_ref[...] = _dot(h, wv_ref[...]).astype(BF16)

    @pl.when(pl.program_id(1) == 0)
    def _():
        h = _rmsnorm(x_ref[...], g_ref[...]).astype(BF16)
        h_ref[...] = h
        project(h)

    @pl.when(pl.program_id(1) > 0)
    def _():
        project(h_ref[...])


def _qkv_proj(x, g, w3, layer, gq, gk):
    m, d = x.shape
    nj = d // TN_PROJ
    out = jax.ShapeDtypeStruct((m, d), BF16)

    def wspec(part):
        return pl.BlockSpec((None, d, TN_PROJ),
                            lambda i, j: (layer, 0, part * nj + j))

    ospec = pl.BlockSpec((TM_PROJ, TN_PROJ), lambda i, j: (i, j))
    return pl.pallas_call(
        _qkv_kernel,
        grid=(m // TM_PROJ, nj),
        in_specs=[
            pl.BlockSpec((TM_PROJ, d), lambda i, j: (i, 0)),
            pl.BlockSpec((1, d), lambda i, j: (0, 0)),
            wspec(0), wspec(1), wspec(2),
            pl.BlockSpec((1, LANES), lambda i, j: (0, 0)),
            pl.BlockSpec((1, LANES), lambda i, j: (0, 0)),
        ],
        out_specs=[ospec, ospec, ospec],
        out_shape=[out, out, out],
        scratch_shapes=[pltpu.VMEM((TM_PROJ, d), BF16)],
        compiler_params=_params("arbitrary", "arbitrary"),
        name="qkv_proj",
    )(x, g, w3, w3, w3, gq, gk)


def _sb_logits(q_ref, k_ref, row0, nrows, k0, nkeys):
    q = q_ref[row0:row0 + nrows, :]
    kblk = k_ref[pl.ds(k0, nkeys), :]
    return lax.dot_general(q, kblk, (((1,), (1,)), ((), ())),
                           preferred_element_type=F32) * (q.shape[1] ** -0.5 * LOG2E)


def _sb_suffix_sums(w, uj, masked):
    groups = []
    for sb in range(w.shape[1] // LANES):
        ws = w[:, sb * LANES:(sb + 1) * LANES]
        lse = jnp.log2(1.0 + jnp.exp2(-jnp.abs(ws)))
        log_beta = jnp.minimum(ws, 0.0) - lse
        log_keep = log_beta - ws
        mask = None
        if masked:
            t_idx = lax.broadcasted_iota(jnp.int32, ws.shape, 0)
            s_idx = lax.broadcasted_iota(jnp.int32, ws.shape, 1) + sb * LANES
            mask = s_idx < t_idx
            log_keep = jnp.where(mask, log_keep, 0.0)
        hi = log_keep.astype(BF16)
        lo = (log_keep - hi.astype(F32)).astype(BF16)
        ct = _dot(jnp.concatenate([hi, lo], axis=1), uj)
        groups.append((log_beta, ct, mask))
    return groups


def _sb_weights(r, groups):
    parts = [None] * len(groups)
    for sb in reversed(range(len(groups))):
        log_beta, ct, mask = groups[sb]
        suffix, total = ct[:, :LANES], ct[:, LANES:]
        a = jnp.exp2(log_beta + (suffix if r is None else r + suffix))
        if mask is not None:
            a = jnp.where(mask, a, 0.0)
        parts[sb] = a.astype(BF16)
        r = total if r is None else r + total
    return r, jnp.concatenate(parts, axis=1)


def _sb_tile(head, uj, row0, nrows, k0, nkeys):
    q_ref, k_ref, v_ref, r_ref, acc_ref = head
    rows = slice(row0, row0 + nrows)
    groups = _sb_suffix_sums(_sb_logits(q_ref, k_ref, row0, nrows, k0, nkeys), uj, False)
    r, a = _sb_weights(r_ref[rows, :], groups)
    r_ref[rows, :] = r
    acc_ref[rows, :] += _dot(a, v_ref[pl.ds(k0, nkeys), :])


def _sb_band(heads, uj, d0, has_history):
    tiles = []
    for head in heads:
        for rt in range(TQ // TD):
            kts = [rt] + ([rt - 1] if rt > 0 or has_history else [])
            tiles.append((head, rt * TD, [pl.multiple_of(d0 + kt * TD, TD) for kt in kts]))
    logits = [[_sb_logits(head[0], head[1], row0, TD, k0, TD) for k0 in k0s]
              for head, row0, k0s in tiles]
    groups = [[_sb_suffix_sums(w, uj, n == 0) for n, w in enumerate(ws)] for ws in logits]
    for (head, row0, k0s), tile_groups in zip(tiles, groups):
        _, _, v_ref, r_ref, acc_ref = head
        r, acc = None, None
        for k0, g in zip(k0s, tile_groups):
            r, a = _sb_weights(r, g)
            pv = _dot(a, v_ref[pl.ds(k0, TD), :])
            acc = pv if acc is None else acc + pv
        r_ref[row0:row0 + TD, :] = r
        acc_ref[row0:row0 + TD, :] = acc


def _sb_kernel(q_ref, k_ref, v_ref, uj_ref, o_ref, acc_ref, r_ref):
    i = pl.program_id(2)
    uj = uj_ref[...]
    dh = acc_ref.shape[2]
    nt = TQ // TD
    d0 = pl.multiple_of(i * TQ, TQ)
    heads = []
    for hd in range(HEADS_PER_STEP):
        cols = slice(hd * dh, (hd + 1) * dh)
        heads.append((q_ref.at[:, cols], k_ref.at[:, cols], v_ref.at[:, cols],
                      r_ref.at[hd], acc_ref.at[hd]))

    @pl.when(i == 0)
    def _():
        _sb_band(heads, uj, d0, False)

    @pl.when(i > 0)
    def _():
        _sb_band(heads, uj, d0, True)

    for hd, head in enumerate(heads):
        def live():
            return jnp.max(r_ref[hd]) > DEAD_STICK_LOG2

        @pl.when(jnp.logical_and(i > 0, live()))
        def _():
            for rt in range(1, nt):
                for kt in reversed(range(-1, rt - 1)):
                    _sb_tile(head, uj, rt * TD, TD, pl.multiple_of(d0 + kt * TD, TD), TD)

            def cond(carry):
                kb, alive = carry
                return jnp.logical_and(kb >= 0, alive)

            def body(carry):
                kb, _ = carry
                _sb_tile(head, uj, 0, TQ, pl.multiple_of(kb * TD, TD), TD)
                return kb - 1, live()

            lax.while_loop(cond, body, (i * nt - 2, live()))

        o_ref[:, hd * dh:(hd + 1) * dh] = acc_ref[hd].astype(BF16)


def _suffix_sum_matrix():
    j = lax.broadcasted_iota(jnp.int32, (2 * LANES, 2 * LANES), 0) % LANES
    s = lax.broadcasted_iota(jnp.int32, (2 * LANES, 2 * LANES), 1)
    return jnp.where((s >= LANES) | (j > s), 1.0, 0.0).astype(BF16)


def _sb_attention(q, k, v, batch, seq):
    m, d = q.shape
    dh = d // N_HEADS
    dw = HEADS_PER_STEP * dh
    q3, k3, v3 = (t.reshape(batch, seq, d) for t in (q, k, v))
    kv_spec = pl.BlockSpec((None, seq, dw), lambda b, h, i: (b, 0, h))
    qo_spec = pl.BlockSpec((None, TQ, dw), lambda b, h, i: (b, i, h))
    o = pl.pallas_call(
        _sb_kernel,
        grid=(batch, N_HEADS // HEADS_PER_STEP, seq // TQ),
        in_specs=[qo_spec, kv_spec, kv_spec,
                  pl.BlockSpec((2 * LANES, 2 * LANES), lambda b, h, i: (0, 0))],
        out_specs=qo_spec,
        out_shape=jax.ShapeDtypeStruct((batch, seq, d), BF16),
        scratch_shapes=[pltpu.VMEM((HEADS_PER_STEP, TQ, dh), F32),
                        pltpu.VMEM((HEADS_PER_STEP, TQ, LANES), F32)],
        compiler_params=_params("arbitrary", "arbitrary", "arbitrary"),
        name="sb_attention",
    )(q3, k3, v3, _suffix_sum_matrix())
    return o.reshape(m, d)


def _proj_res_kernel(a_ref, w_ref, x_hbm, o_ref, x_buf, sems):
    i = pl.program_id(0)
    nsteps = pl.num_programs(0)

    def copy(tile, slot):
        rows = pl.ds(pl.multiple_of(tile * TM_RES, TM_RES), TM_RES)
        return pltpu.make_async_copy(x_hbm.at[rows, :], x_buf.at[slot], sems.at[slot])

    @pl.when(i == 0)
    def _():
        for ahead in range(RES_SLOTS - 1):
            copy(ahead, ahead).start()

    nxt = i + (RES_SLOTS - 1)

    @pl.when(nxt < nsteps)
    def _():
        copy(nxt, nxt % RES_SLOTS).start()

    slot = i % RES_SLOTS
    copy(i, slot).wait()
    o_ref[...] = x_buf[slot] + _dot(a_ref[...], w_ref[...])


def _proj_residual(a, w, layer, x):
    m, d = x.shape
    return pl.pallas_call(
        _proj_res_kernel,
        grid=(m // TM_RES,),
        in_specs=[
            pl.BlockSpec((TM_RES, d), lambda i: (i, 0)),
            pl.BlockSpec((None, d, d), lambda i: (layer, 0, 0)),
            pl.BlockSpec(memory_space=pl.ANY),
        ],
        out_specs=pl.BlockSpec((TM_RES, d), lambda i: (i, 0)),
        out_shape=jax.ShapeDtypeStruct((m, d), F32),
        scratch_shapes=[pltpu.VMEM((RES_SLOTS, TM_RES, d), F32),
                        pltpu.SemaphoreType.DMA((RES_SLOTS,))],
        compiler_params=_params("arbitrary"),
        name="proj_residual",
    )(a, w, x)


def _pool_kernel(x_ref, g_ref, w_ref, sc_ref, o_ref, buf_ref):
    j = pl.program_id(1)
    ts, d = x_ref.shape
    gw = d // len(POOL_WINDOWS)

    @pl.when(j == 0)
    def _():
        buf_ref[0:HALO, :] = jnp.zeros((HALO, d), F32)

    @pl.when(j > 0)
    def _():
        buf_ref[0:HALO, :] = buf_ref[ts:ts + HALO, :]

    x = x_ref[...]
    buf_ref[HALO:HALO + ts, :] = _rmsnorm(x, g_ref[...])
    pos1 = j * ts + 1 + lax.broadcasted_iota(jnp.int32, (ts, 1), 0)
    for gi, win in enumerate(POOL_WINDOWS):
        cols = slice(gi * gw, (gi + 1) * gw)
        ext = buf_ref[:, cols]
        tot = ext
        lag = 1
        while lag < win:
            tot = tot + pltpu.roll(tot, lag, 0)
            lag *= 2
        tot = tot[HALO:, :]
        cur = ext[HALO:, :]
        cnt = jnp.minimum(pos1, win).astype(F32)
        piece = (tot / cnt - cur).astype(BF16)
        y = _dot(piece, w_ref[gi])
        o_ref[:, cols] = x[:, cols] + y * sc_ref[:, cols]


def _pool_mixer(x, g, w, layer, sc, batch, seq):
    m, d = x.shape
    ng = len(POOL_WINDOWS)
    gw = d // ng
    nt = seq // TS_POOL
    xspec = pl.BlockSpec((TS_POOL, d), lambda b, j: (b * nt + j, 0))
    vspec = pl.BlockSpec((1, d), lambda b, j: (0, 0))
    return pl.pallas_call(
        _pool_kernel,
        grid=(batch, nt),
        in_specs=[xspec, vspec,
                  pl.BlockSpec((None, ng, gw, gw), lambda b, j: (layer, 0, 0, 0)),
                  vspec],
        out_specs=xspec,
        out_shape=jax.ShapeDtypeStruct((m, d), F32),
        scratch_shapes=[pltpu.VMEM((HALO + TS_POOL, d), F32)],
        compiler_params=_params("arbitrary", "arbitrary"),
        name="pool_mixer",
    )(x, g, w, sc)


def _conv_in_kernel(tiles_per_seq, x_ref, g_ref, wb_ref, wc_ref, wu_ref, cw_ref,
                    o_ref, h_ref, carry_ref):
    i = pl.program_id(0)
    j = pl.program_id(1)

    @pl.when(i % tiles_per_seq == 0)
    def _():
        carry_ref[j] = jnp.zeros(carry_ref.shape[1:], F32)

    def gated_conv(h):
        cu = _dot(h, wc_ref[...]) * _dot(h, wu_ref[...])
        tm = cu.shape[0]
        prev = carry_ref[j]
        p1 = prev[SUBLANES - 1:SUBLANES, :]
        p2 = prev[SUBLANES - 2:SUBLANES - 1, :]
        rows = lax.broadcasted_iota(jnp.int32, cu.shape, 0)
        lag1 = jnp.where(rows == 0, p1, pltpu.roll(cu, 1, 0))
        lag2 = jnp.where(rows == 0, p2, jnp.where(rows == 1, p1, pltpu.roll(cu, 2, 0)))
        cw = cw_ref[...]
        y = cw[0:1, :] * lag2 + cw[1:2, :] * lag1 + cw[2:3, :] * cu
        carry_ref[j] = cu[tm - SUBLANES:tm, :]
        o_ref[...] = (_dot(h, wb_ref[...]) * y).astype(BF16)

    @pl.when(j == 0)
    def _():
        h = _rmsnorm(x_ref[...], g_ref[...]).astype(BF16)
        h_ref[...] = h
        gated_conv(h)

    @pl.when(j > 0)
    def _():
        gated_conv(h_ref[...])


def _conv_in(x, g, w3, layer, cw, seq):
    m, d = x.shape
    nj = d // TN_PROJ

    def wspec(part):
        return pl.BlockSpec((None, d, TN_PROJ),
                            lambda i, j: (layer, 0, part * nj + j))

    return pl.pallas_call(
        functools.partial(_conv_in_kernel, seq // TM_PROJ),
        grid=(m // TM_PROJ, nj),
        in_specs=[
            pl.BlockSpec((TM_PROJ, d), lambda i, j: (i, 0)),
            pl.BlockSpec((1, d), lambda i, j: (0, 0)),
            wspec(0), wspec(1), wspec(2),
            pl.BlockSpec((None, CONV_W, TN_PROJ), lambda i, j: (layer, 0, j)),
        ],
        out_specs=pl.BlockSpec((TM_PROJ, TN_PROJ), lambda i, j: (i, j)),
        out_shape=jax.ShapeDtypeStruct((m, d), BF16),
        scratch_shapes=[pltpu.VMEM((TM_PROJ, d), BF16),
                        pltpu.VMEM((nj, SUBLANES, TN_PROJ), F32)],
        compiler_params=_params("arbitrary", "arbitrary"),
        name="conv_in",
    )(x, g, w3, w3, w3, cw)


def _ffn_step(x_ref, g_ref, weights, o_ref, h_ref):
    def hidden_tile(h):
        wg, wu, wd = weights()
        gate = _dot(h, wg)
        up = _dot(h, wu)
        act = (gate * jax.nn.sigmoid(gate) * up).astype(BF16)
        return _dot(act, wd)

    @pl.when(pl.program_id(1) == 0)
    def _():
        x = x_ref[...]
        h = _rmsnorm(x, g_ref[...]).astype(BF16)
        h_ref[...] = h
        o_ref[...] = x + hidden_tile(h)

    @pl.when(pl.program_id(1) > 0)
    def _():
        o_ref[...] += hidden_tile(h_ref[...])


def _ffn_head_kernel(layer, x_ref, g_ref, wg_hbm, wu_hbm, wd_hbm,
                     o_ref, wg_out, wu_out, wd_out, h_ref, wg_buf, wu_buf, wd_buf, sems):
    j = pl.program_id(1)
    nsteps = pl.num_programs(1)

    def copies(tile, slot):
        cols = pl.ds(pl.multiple_of(tile * TF_HEAD, TF_HEAD), TF_HEAD)
        return (
            pltpu.make_async_copy(wg_hbm.at[layer, :, cols], wg_buf.at[slot], sems.at[0, slot]),
            pltpu.make_async_copy(wu_hbm.at[layer, :, cols], wu_buf.at[slot], sems.at[1, slot]),
            pltpu.make_async_copy(wd_hbm.at[layer, cols, :], wd_buf.at[slot], sems.at[2, slot]),
        )

    @pl.when(j == 0)
    def _():
        for ahead in range(HEAD_SLOTS - 1):
            for c in copies(ahead, ahead):
                c.start()

    nxt = j + (HEAD_SLOTS - 1)

    @pl.when(nxt < nsteps)
    def _():
        for c in copies(nxt, nxt % HEAD_SLOTS):
            c.start()

    slot = j % HEAD_SLOTS
    for c in copies(j, slot):
        c.wait()

    def weights():
        wg, wu, wd = (w[slot].astype(BF16) for w in (wg_buf, wu_buf, wd_buf))
        wg_out[...], wu_out[...], wd_out[...] = wg, wu, wd
        return wg, wu, wd

    _ffn_step(x_ref, g_ref, weights, o_ref, h_ref)


def _ffn_kernel(x_ref, g_ref, wg_ref, wu_ref, wd_ref, o_ref, h_ref):
    _ffn_step(x_ref, g_ref, lambda: (wg_ref[...], wu_ref[...], wd_ref[...]), o_ref, h_ref)


def _ffn(x, g, wg32, wu32, wd32, layer):
    m, d = x.shape
    f = wg32.shape[2]
    once = pl.Buffered(1)
    hbm = pl.BlockSpec(memory_space=pl.ANY)
    y, wg, wu, wd = pl.pallas_call(
        functools.partial(_ffn_head_kernel, layer),
        grid=(1, f // TF_HEAD),
        in_specs=[
            pl.BlockSpec((TM_FFN, d), lambda i, j: (0, 0), pipeline_mode=once),
            pl.BlockSpec((1, d), lambda i, j: (0, 0)),
            hbm, hbm, hbm,
        ],
        out_specs=[
            pl.BlockSpec((TM_FFN, d), lambda i, j: (0, 0)),
            pl.BlockSpec((d, TF_HEAD), lambda i, j: (0, j)),
            pl.BlockSpec((d, TF_HEAD), lambda i, j: (0, j)),
            pl.BlockSpec((TF_HEAD, d), lambda i, j: (j, 0)),
        ],
        out_shape=[jax.ShapeDtypeStruct((m, d), F32),
                   jax.ShapeDtypeStruct((d, f), BF16),
                   jax.ShapeDtypeStruct((d, f), BF16),
                   jax.ShapeDtypeStruct((f, d), BF16)],
        input_output_aliases={0: 0},
        scratch_shapes=[pltpu.VMEM((TM_FFN, d), BF16),
                        pltpu.VMEM((HEAD_SLOTS, d, TF_HEAD), F32),
                        pltpu.VMEM((HEAD_SLOTS, d, TF_HEAD), F32),
                        pltpu.VMEM((HEAD_SLOTS, TF_HEAD, d), F32),
                        pltpu.SemaphoreType.DMA((3, HEAD_SLOTS))],
        compiler_params=_params("arbitrary", "arbitrary"),
        name="ffn_head",
    )(x, g, wg32, wu32, wd32)
    return pl.pallas_call(
        _ffn_kernel,
        grid=(m // TM_FFN - 1, f // TF_FFN),
        in_specs=[
            pl.BlockSpec((TM_FFN, d), lambda i, j: (i + 1, 0)),
            pl.BlockSpec((1, d), lambda i, j: (0, 0)),
            pl.BlockSpec((d, TF_FFN), lambda i, j: (0, j)),
            pl.BlockSpec((d, TF_FFN), lambda i, j: (0, j)),
            pl.BlockSpec((TF_FFN, d), lambda i, j: (j, 0)),
        ],
        out_specs=pl.BlockSpec((TM_FFN, d), lambda i, j: (i + 1, 0)),
        out_shape=jax.ShapeDtypeStruct((m, d), F32),
        input_output_aliases={0: 0},
        scratch_shapes=[pltpu.VMEM((TM_FFN, d), BF16)],
        compiler_params=_params("arbitrary", "arbitrary"),
        name="ffn",
    )(y, g, wg, wu, wd)


def kernel(x, norm_mix_g, norm_ffn_g, sb_w_qkv, sb_g_q, sb_g_k, sb_w_o, pool_w, pool_scale, conv_w_in, conv_w, conv_w_out, ffn_w_gate, ffn_w_up, ffn_w_down):
    batch, seq, d = x.shape
    depth = norm_mix_g.shape[0]
    f = ffn_w_gate.shape[2]
    assert x.dtype == F32 and d // N_HEADS == LANES and N_HEADS % HEADS_PER_STEP == 0
    assert d % TN_PROJ == 0 and d % len(POOL_WINDOWS) == 0 and f % TF_FFN == 0 and f % TF_HEAD == 0
    assert all(seq % t == 0 for t in (TQ, TM_PROJ, TS_POOL)) and TQ % TD == 0
    assert all((batch * seq) % t == 0 for t in (TM_PROJ, TM_RES, TM_FFN)) and batch * seq > TM_FFN
    xf = x.reshape(batch * seq, d)
    sb_w_qkv, sb_w_o, pool_w, conv_w_in, conv_w_out = (
        w.astype(BF16) for w in (sb_w_qkv, sb_w_o, pool_w, conv_w_in, conv_w_out))
    for i in range(depth):
        kind, j = i % N_MIXERS, i // N_MIXERS
        g_mix = norm_mix_g[i:i + 1]
        if kind == 0:
            q, k, v = _qkv_proj(xf, g_mix, sb_w_qkv, j, sb_g_q[j:j + 1], sb_g_k[j:j + 1])
            o = _sb_attention(q, k, v, batch, seq)
            xf = _proj_residual(o, sb_w_o, j, xf)
        elif kind == 1:
            xf = _pool_mixer(xf, g_mix, pool_w, j, pool_scale[j:j + 1], batch, seq)
        else:
            gated = _conv_in(xf, g_mix, conv_w_in, j, conv_w, seq)
            xf = _proj_residual(gated, conv_w_out, j, xf)
        xf = _ffn(xf, norm_ffn_g[i:i + 1], ffn_w_gate, ffn_w_up, ffn_w_down, i)
    return xf.reshape(batch, seq, d)
```
